```python
import math
import jax, jax.numpy as jnp
from jax import lax
import numpy as np

D_MODEL = 1024
BATCH = 8
SEQ = 8192
DEPTH = 1

N_HEADS = 8
HEAD_DIM = 64
V_DIM = 2 * HEAD_DIM
QK_WIDTH = N_HEADS * 2 * HEAD_DIM
D_ATTN = N_HEADS * V_DIM
Q_BLOCK = 128
ROPE_THETA = 10000.0
D_RNN = D_MODEL
N_RNN_BLOCKS = 8
RNN_BLOCK = D_RNN // N_RNN_BLOCKS
CONV_WIDTH = 4
LRU_C = 8.0
D_FF = 4 * D_MODEL
N_BRANCH = 2
IN_COLS = 2 * D_RNN + 2 * QK_WIDTH + D_ATTN + N_BRANCH * D_MODEL
N_MOD = 6
NORM_EPS = 1e-6

kernel_name = "hybrid_rglru_diffattn_gated_block"


def lambda_init(layer_idx):
    return 0.8 - 0.6 * math.exp(-0.3 * layer_idx)


def rms_norm(x):
    xf = x.astype(jnp.float32)
    return (xf * lax.rsqrt(jnp.mean(xf * xf, axis=-1, keepdims=True) + NORM_EPS)).astype(x.dtype)


def rope_tables(positions):
    inv_freq = ROPE_THETA ** (-jnp.arange(0, HEAD_DIM, 2, dtype=jnp.float32) / HEAD_DIM)
    ang = positions.astype(jnp.float32)[..., None] * inv_freq
    return jnp.cos(ang)[:, :, None, None, :], jnp.sin(ang)[:, :, None, None, :]


def apply_rope(x, cos, sin):
    xf = x.astype(jnp.float32)
    x1, x2 = xf[..., : HEAD_DIM // 2], xf[..., HEAD_DIM // 2:]
    return jnp.concatenate([x1 * cos - x2 * sin, x2 * cos + x1 * sin], axis=-1).astype(x.dtype)


def causal_depthwise_conv(x, w, b):
    y = lax.conv_general_dilated(
        x, w[:, None, :].astype(x.dtype), window_strides=(1,), padding=[(CONV_WIDTH - 1, 0)],
        dimension_numbers=("NWC", "WIO", "NWC"), feature_group_count=x.shape[-1])
    return y + b


def rg_lru(x, w_a, b_a, w_x, b_x, lam):
    B, S, C = x.shape
    xb = x.reshape(B, S, N_RNN_BLOCKS, RNN_BLOCK)
    r = jax.nn.sigmoid(jnp.einsum("bsnk,nkj->bsnj", xb, w_a).reshape(B, S, C) + b_a)
    i = jax.nn.sigmoid(jnp.einsum("bsnk,nkj->bsnj", xb, w_x).reshape(B, S, C) + b_x)
    log_a = -LRU_C * r.astype(jnp.float32) * jax.nn.softplus(-lam.astype(jnp.float32))
    a = jnp.exp(log_a)
    u = jnp.sqrt(-jnp.expm1(2.0 * log_a)) * (i * x).astype(jnp.float32)

    def combine(left, right):
        a1, b1 = left
        a2, b2 = right
        return a1 * a2, a2 * b1 + b2

    _, h = lax.associative_scan(combine, (a, u), axis=1)
    return h.astype(x.dtype)


def diff_attention(q, k, v, lam, subln_gain, lam_init):
    B, S = q.shape[0], q.shape[1]
    nb = S // Q_BLOCK
    scale = HEAD_DIM ** -0.5
    qb = q.reshape(B, nb, Q_BLOCK, N_HEADS, 2, HEAD_DIM).transpose(1, 0, 3, 4, 2, 5)
    kt = k.transpose(0, 2, 3, 1, 4)
    vt = v.transpose(0, 2, 1, 3)
    key_pos = jnp.arange(S)

    def one_block(args):
        qi, blk = args
        s = jnp.einsum("bhcqd,bhckd->bhcqk", qi, kt).astype(jnp.float32) * scale
        q_pos = blk * Q_BLOCK + jnp.arange(Q_BLOCK)
        mask = key_pos[None, :] <= q_pos[:, None]
        p = jax.nn.softmax(jnp.where(mask, s, -jnp.inf), axis=-1)
        attn = p[:, :, 0] - lam * p[:, :, 1]
        return jnp.einsum("bhqk,bhkv->bhqv", attn.astype(vt.dtype), vt)

    o = lax.map(one_block, (qb, jnp.arange(nb)))
    o = o.transpose(1, 0, 3, 2, 4).reshape(B, S, N_HEADS, V_DIM)
    o = rms_norm(o) * subln_gain * (1.0 - lam_init)
    return o.reshape(B, S, D_ATTN)


def setup_inputs(seed: int = 0) -> dict:
    key = jax.random.key(seed)
    ks = jax.random.split(key, 26)
    f32 = jnp.float32
    nrm = lambda k, shape, s: jax.random.normal(k, shape, f32) * s
    a_c = jax.random.uniform(ks[11], (DEPTH, D_RNN), f32, 0.9, 0.999)
    p = a_c ** (1.0 / LRU_C)
    return {
        "x": nrm(ks[0], (BATCH, SEQ, D_MODEL), 1.0),
        "c": nrm(ks[1], (BATCH, D_MODEL), 1.0),
        "positions": (jnp.arange(SEQ, dtype=jnp.int32)[None, :]
                      + jax.random.randint(ks[2], (BATCH, 1), 0, 1024, dtype=jnp.int32)),
        "w_ada": nrm(ks[3], (DEPTH, D_MODEL, N_MOD * D_MODEL), 0.5 * D_MODEL ** -0.5),
        "b_ada": nrm(ks[4], (DEPTH, N_MOD * D_MODEL), 0.01),
        "w_in": nrm(ks[5], (DEPTH, D_MODEL, IN_COLS), D_MODEL ** -0.5),
        "conv_w": nrm(ks[6], (DEPTH, CONV_WIDTH, D_RNN), CONV_WIDTH ** -0.5),
        "conv_b": nrm(ks[7], (DEPTH, D_RNN), 0.01),
        "rglru_wa": nrm(ks[8], (DEPTH, N_RNN_BLOCKS, RNN_BLOCK, RNN_BLOCK), RNN_BLOCK ** -0.5),
        "rglru_ba": nrm(ks[9], (DEPTH, D_RNN), 0.01),
        "rglru_wx": nrm(ks[10], (DEPTH, N_RNN_BLOCKS, RNN_BLOCK, RNN_BLOCK), RNN_BLOCK ** -0.5),
        "rglru_bx": nrm(ks[12], (DEPTH, D_RNN), 0.01),
        "rglru_lambda": jnp.log(p) - jnp.log1p(-p),
        "q_norm_gain": 1.0 + nrm(ks[13], (DEPTH, HEAD_DIM), 0.02),
        "k_norm_gain": 1.0 + nrm(ks[14], (DEPTH, HEAD_DIM), 0.02),
        "lambda_q1": nrm(ks[15], (DEPTH, HEAD_DIM), 0.1),
        "lambda_k1": nrm(ks[16], (DEPTH, HEAD_DIM), 0.1),
        "lambda_q2": nrm(ks[17], (DEPTH, HEAD_DIM), 0.1),
        "lambda_k2": nrm(ks[18], (DEPTH, HEAD_DIM), 0.1),
        "subln_gain": 1.0 + nrm(ks[19], (DEPTH, V_DIM), 0.02),
        "w_proj_rnn": nrm(ks[20], (DEPTH, D_RNN, D_MODEL), D_RNN ** -0.5),
        "w_proj_attn": nrm(ks[21], (DEPTH, D_ATTN, D_MODEL), D_ATTN ** -0.5),
        "w_out": nrm(ks[22], (DEPTH, D_MODEL, D_MODEL), D_MODEL ** -0.5),
        "w_ff1": nrm(ks[23], (DEPTH, D_MODEL, D_FF), D_MODEL ** -0.5),
        "w_ff2": nrm(ks[24], (DEPTH, D_FF, D_MODEL), D_FF ** -0.5),
    }


def reference(x, c, positions, w_ada, b_ada, w_in, conv_w, conv_b, rglru_wa, rglru_ba, rglru_wx,
              rglru_bx, rglru_lambda, q_norm_gain, k_norm_gain, lambda_q1, lambda_k1, lambda_q2,
              lambda_k2, subln_gain, w_proj_rnn, w_proj_attn, w_out, w_ff1, w_ff2):
    B, S, _ = x.shape
    cos, sin = rope_tables(positions)
    c_act = jax.nn.silu(c)
    o1 = D_RNN
    o2 = o1 + D_RNN
    o3 = o2 + QK_WIDTH
    o4 = o3 + QK_WIDTH
    o5 = o4 + D_ATTN
    for l in range(DEPTH):
        lam_init = lambda_init(l)
        mod = (c_act @ w_ada[l] + b_ada[l])[:, None, :]
        shift1, scale1, gate1, shift2, scale2, gate2 = jnp.split(mod, N_MOD, axis=-1)

        h = rms_norm(x) * (1.0 + scale1) + shift1
        proj = h @ w_in[l]
        xr, gr, q, k, v, gm = jnp.split(proj, [o1, o2, o3, o4, o5], axis=-1)

        xr = causal_depthwise_conv(xr, conv_w[l], conv_b[l])
        y_rnn = rg_lru(xr, rglru_wa[l], rglru_ba[l], rglru_wx[l], rglru_bx[l], rglru_lambda[l]) * jax.nn.gelu(gr)

        q = apply_rope(rms_norm(q.reshape(B, S, N_HEADS, 2, HEAD_DIM)) * q_norm_gain[l], cos, sin)
        k = apply_rope(rms_norm(k.reshape(B, S, N_HEADS, 2, HEAD_DIM)) * k_norm_gain[l], cos, sin)
        v = v.reshape(B, S, N_HEADS, V_DIM)
        lam = (jnp.exp(jnp.sum(lambda_q1[l] * lambda_k1[l]).astype(jnp.float32))
               - jnp.exp(jnp.sum(lambda_q2[l] * lambda_k2[l]).astype(jnp.float32)) + lam_init)
        y_attn = diff_attention(q, k, v, lam, subln_gain[l], lam_init)

        gates = jax.nn.sigmoid(gm).reshape(B, S, N_BRANCH, D_MODEL)
        merged = gates[:, :, 0] * (y_rnn @ w_proj_rnn[l]) + gates[:, :, 1] * (y_attn @ w_proj_attn[l])
        x = x + gate1 * (merged @ w_out[l])

        h2 = rms_norm(x) * (1.0 + scale2) + shift2
        ff = jnp.square(jax.nn.relu(h2 @ w_ff1[l])) @ w_ff2[l]
        x = x + gate2 * ff
    return x
```

```python
import functools
import math

import jax
import jax.numpy as jnp
from jax import lax
from jax.experimental import pallas as pl
from jax.experimental.pallas import tpu as pltpu

N_HEADS = 8
HEAD_DIM = 64
V_DIM = 2 * HEAD_DIM
ROPE_THETA = 10000.0
N_RNN_BLOCKS = 8
CONV_WIDTH = 4
LRU_C = 8.0
N_BRANCH = 2
N_MOD = 6
NORM_EPS = 1e-6

V7X_SUBLANES = 8
V7X_LANES = 128
V7X_VMEM_LIMIT_BYTES = 56 * 1024 * 1024

F32 = jnp.float32
BF16 = jnp.bfloat16


def _dot(a, b):
    return jnp.dot(a, b, preferred_element_type=F32)


def _dot_nt(a, b):
    return lax.dot_general(a, b, (((1,), (1,)), ((), ())), preferred_element_type=F32)


def _resident(shape):
    zeros = (0,) * len(shape)
    return pl.BlockSpec(shape, lambda *_: zeros, pipeline_mode=pl.Buffered(1))


def _params(semantics):
    return pltpu.CompilerParams(dimension_semantics=semantics, vmem_limit_bytes=V7X_VMEM_LIMIT_BYTES)


def _mod_kernel(c_ref, w_ref, b_ref, o_ref):
    c = c_ref[...]
    ca = c * jax.nn.sigmoid(c)
    w = w_ref[...]
    ca_hi = ca.astype(BF16)
    ca_lo = (ca - ca_hi.astype(F32)).astype(BF16)
    w_hi = w.astype(BF16)
    w_lo = (w - w_hi.astype(F32)).astype(BF16)
    acc = _dot(ca_hi, w_hi) + _dot(ca_lo, w_hi) + _dot(ca_hi, w_lo)
    o_ref[...] = acc + b_ref[...]


def _modulation(c, w_ada, b_ada):
    B, D = c.shape
    n = w_ada.shape[1]
    tn = n // 4
    return pl.pallas_call(
        _mod_kernel,
        grid=(n // tn,),
        in_specs=[
            pl.BlockSpec((B, D), lambda j: (0, 0)),
            pl.BlockSpec((D, tn), lambda j: (0, j)),
            pl.BlockSpec((1, tn), lambda j: (0, j)),
        ],
        out_specs=pl.BlockSpec((B, tn), lambda j: (0, j)),
        out_shape=jax.ShapeDtypeStruct((B, n), F32),
        compiler_params=_params(("arbitrary",)),
        name="modulation",
    )(c, w_ada, b_ada.reshape(1, n))


def _rope_kernel(pos_ref, invf_ref, cos_ref, sin_ref):
    ang = invf_ref[...] * pos_ref[...].astype(F32)
    cos_ref[...] = jnp.cos(ang)
    sin_ref[...] = jnp.sin(ang)


def _rope_tables(positions):
    B, S = positions.shape
    half = HEAD_DIM // 2
    inv_freq = (ROPE_THETA ** (-jnp.arange(0, HEAD_DIM, 2, dtype=F32) / HEAD_DIM)).reshape(half, 1)
    ts = min(S, 2048)
    out = jax.ShapeDtypeStruct((B, half, S), F32)
    return pl.pallas_call(
        _rope_kernel,
        grid=(B, S // ts),
        in_specs=[
            pl.BlockSpec((None, 1, ts), lambda b, i: (b, 0, i)),
            pl.BlockSpec((half, 1), lambda b, i: (0, 0)),
        ],
        out_specs=[pl.BlockSpec((None, half, ts), lambda b, i: (b, 0, i))] * 2,
        out_shape=[out, out],
        compiler_params=_params(("arbitrary", "arbitrary")),
        name="rope_tables",
    )(positions.reshape(B, 1, S), inv_freq)


def _ada_norm(x, shift, scale):
    ms = jnp.mean(x * x, axis=-1, keepdims=True)
    return (x * lax.rsqrt(ms + NORM_EPS)) * (1.0 + scale) + shift


def _norm_rope_fm(t, gain, cos, sin):
    half = HEAD_DIM // 2
    outs = []
    for c in range(2):
        blk = t[c * HEAD_DIM:(c + 1) * HEAD_DIM, :]
        ms = jnp.mean(blk * blk, axis=0, keepdims=True)
        n = blk * lax.rsqrt(ms + NORM_EPS) * gain
        x1 = n[:half, :]
        x2 = n[half:, :]
        outs.append(x1 * cos - x2 * sin)
        outs.append(x2 * cos + x1 * sin)
    return jnp.concatenate(outs, axis=0)


def _inproj_kernel(x_ref, mod_ref, wa_ref, wt_ref, gq_ref, gk_ref, cos_ref, sin_ref,
                   xr_ref, gr_ref, gm_ref, qt_ref, k_ref, vt_ref, *, d_rnn, d_qk, d_attn):
    h = _ada_norm(x_ref[...], mod_ref[0:1, :], mod_ref[1:2, :]).astype(BF16)

    chunk = d_rnn
    col = 0
    for ref in (xr_ref, gr_ref, gm_ref):
        for c0 in range(0, ref.shape[-1], chunk):
            ref[:, c0:c0 + chunk] = _dot(h, wa_ref[:, col:col + chunk]).astype(ref.dtype)
            col += chunk

    cos = cos_ref[...]
    sin = sin_ref[...]
    hd2 = 2 * HEAD_DIM
    for hh in range(N_HEADS):
        qh = _dot_nt(wt_ref[hh * hd2:(hh + 1) * hd2, :], h)
        qt_ref[hh * hd2:(hh + 1) * hd2, :] = _norm_rope_fm(qh, gq_ref[...], cos, sin).astype(qt_ref.dtype)
    for hh in range(N_HEADS):
        kh = _dot_nt(wt_ref[d_qk + hh * hd2:d_qk + (hh + 1) * hd2, :], h)
        kh = _norm_rope_fm(kh, gk_ref[...], cos, sin)
        k_ref[:, hh * hd2:(hh + 1) * hd2] = kh.T.astype(k_ref.dtype)
    vt_ref[...] = _dot_nt(wt_ref[2 * d_qk:2 * d_qk + d_attn, :], h).astype(vt_ref.dtype)


def _in_projection(x, mod3, w_a, w_t, gq, gk, cos_t, sin_t, *, tm, d_rnn):
    B, S, D = x.shape
    d_qk = N_HEADS * 2 * HEAD_DIM
    d_attn = N_HEADS * V_DIM
    d_gm = N_BRANCH * D
    half = HEAD_DIM // 2
    nt = S // tm
    tok = lambda w: pl.BlockSpec((None, tm, w), lambda b, i: (b, i, 0))
    fm = lambda r: pl.BlockSpec((None, r, tm), lambda b, i: (b, 0, i))
    kernel = functools.partial(_inproj_kernel, d_rnn=d_rnn, d_qk=d_qk, d_attn=d_attn)
    return pl.pallas_call(
        kernel,
        grid=(B, nt),
        in_specs=[
            tok(D),
            pl.BlockSpec((None, N_MOD, D), lambda b, i: (b, 0, 0)),
            _resident(w_a.shape),
            _resident(w_t.shape),
            _resident(gq.shape),
            _resident(gk.shape),
            fm(half),
            fm(half),
        ],
        out_specs=[
            tok(d_rnn), tok(d_rnn), tok(d_gm),
            fm(d_qk),
            tok(d_qk),
            pl.BlockSpec((None, None, d_attn, tm), lambda b, i: (b, i, 0, 0)),
        ],
        out_shape=[
            jax.ShapeDtypeStruct((B, S, d_rnn), BF16),
            jax.ShapeDtypeStruct((B, S, d_rnn), BF16),
            jax.ShapeDtypeStruct((B, S, d_gm), BF16),
            jax.ShapeDtypeStruct((B, d_qk, S), BF16),
            jax.ShapeDtypeStruct((B, S, d_qk), BF16),
            jax.ShapeDtypeStruct((B, nt, d_attn, tm), BF16),
        ],
        compiler_params=_params(("arbitrary", "arbitrary")),
        name="in_projection",
    )(x, mod3, w_a, w_t, gq, gk, cos_t, sin_t)


def _shift_rows(x, prev8, k, row8):
    rolled = pltpu.roll(x, k, 0)
    top = jnp.where(row8 < k, pltpu.roll(prev8, k, 0), rolled[:V7X_SUBLANES, :])
    return jnp.concatenate([top, rolled[V7X_SUBLANES:, :]], axis=0)


def _rglru_kernel(xr_ref, gr_ref, cw_ref, cb_ref, wg_ref, ba_ref, bx_ref, lam_ref,
                  y_ref, tail_ref, h_ref):
    ts, C = xr_ref.shape
    blk = C // N_RNN_BLOCKS

    @pl.when(pl.program_id(1) == 0)
    def _():
        tail_ref[...] = jnp.zeros_like(tail_ref)
        h_ref[...] = jnp.zeros_like(h_ref)

    x = xr_ref[...].astype(F32)
    prev8 = tail_ref[...]
    row8 = lax.broadcasted_iota(jnp.int32, (V7X_SUBLANES, C), 0)
    conv = x * cw_ref[CONV_WIDTH - 1:CONV_WIDTH, :] + cb_ref[...]
    for d in range(1, CONV_WIDTH):
        conv = conv + _shift_rows(x, prev8, d, row8) * cw_ref[CONV_WIDTH - 1 - d:CONV_WIDTH - d, :]
    tail_ref[...] = x[ts - V7X_SUBLANES:, :]

    convb = conv.astype(BF16)
    pre = [_dot(convb[:, n * blk:(n + 1) * blk], wg_ref[n]) for n in range(N_RNN_BLOCKS)]
    r = jax.nn.sigmoid(jnp.concatenate([p[:, :blk] for p in pre], axis=1) + ba_ref[...])
    i = jax.nn.sigmoid(jnp.concatenate([p[:, blk:] for p in pre], axis=1) + bx_ref[...])

    z = -lam_ref[...]
    softplus = jnp.maximum(z, 0.0) + jnp.log1p(jnp.exp(-jnp.abs(z)))
    log_a = (-LRU_C) * r * softplus
    a = jnp.exp(log_a)
    u = jnp.sqrt(-jnp.tanh(log_a) * (a * a + 1.0)) * (i * conv)

    rowg = lax.broadcasted_iota(jnp.int32, (ts, C), 0) % V7X_SUBLANES
    s = 1
    while s < V7X_SUBLANES:
        ok = rowg >= s
        u = jnp.where(ok, u + a * pltpu.roll(u, s, 0), u)
        a = jnp.where(ok, a * pltpu.roll(a, s, 0), a)
        s *= 2
    hprev = h_ref[...]
    hs = []
    for g in range(ts // V7X_SUBLANES):
        sl = slice(g * V7X_SUBLANES, (g + 1) * V7X_SUBLANES)
        hg = u[sl, :] + a[sl, :] * hprev
        hs.append(hg)
        hprev = jnp.broadcast_to(hg[V7X_SUBLANES - 1:, :], hg.shape)
    h_ref[...] = hprev
    h = jnp.concatenate(hs, axis=0)

    y_ref[...] = (h * jax.nn.gelu(gr_ref[...].astype(F32))).astype(y_ref.dtype)


def _rg_lru(xr, gr, conv_w, conv_b, w_gate, b_a, b_x, lam, *, ts):
    B, S, C = xr.shape
    tok = pl.BlockSpec((None, ts, C), lambda b, i: (b, i, 0))
    row = lambda a: _resident(a.shape)
    return pl.pallas_call(
        _rglru_kernel,
        grid=(B, S // ts),
        in_specs=[tok, tok, row(conv_w), row(conv_b), row(w_gate), row(b_a), row(b_x), row(lam)],
        out_specs=tok,
        out_shape=jax.ShapeDtypeStruct((B, S, C), BF16),
        scratch_shapes=[pltpu.VMEM((V7X_SUBLANES, C), F32), pltpu.VMEM((V7X_SUBLANES, C), F32)],
        compiler_params=_params(("arbitrary", "arbitrary")),
        name="rg_lru",
    )(xr, gr, conv_w, conv_b, w_gate, b_a, b_x, lam)


def _attn_kernel(lamv_ref, qt_ref, k_ref, vt_ref, g_ref, o_ref, qbd_ref, m_ref, l_ref, acc_ref,
                 *, tq, tk, lam_init):
    i = pl.program_id(2)

    zero = jnp.zeros((HEAD_DIM, tq), qbd_ref.dtype)
    qbd_ref[:HEAD_DIM, :tq] = qt_ref[:HEAD_DIM, :]
    qbd_ref[:HEAD_DIM, tq:] = zero
    qbd_ref[HEAD_DIM:, :tq] = zero
    qbd_ref[HEAD_DIM:, tq:] = qt_ref[HEAD_DIM:, :]
    m_ref[...] = jnp.full(m_ref.shape, -jnp.inf, F32)
    l_ref[...] = jnp.zeros_like(l_ref)
    acc_ref[...] = jnp.zeros_like(acc_ref)

    def block(j, masked):
        start = pl.multiple_of(j * tk, tk)
        s = _dot(k_ref[pl.ds(start, tk), :], qbd_ref[...])
        if masked:
            key = lax.broadcasted_iota(jnp.int32, s.shape, 0)
            col = lax.broadcasted_iota(jnp.int32, s.shape, 1)
            qry = jnp.where(col >= tq, col - tq, col)
            s = jnp.where(key <= qry, s, -jnp.inf)
        m_old = m_ref[...]
        m_new = jnp.maximum(m_old, jnp.max(s, axis=0, keepdims=True))
        alpha = jnp.exp(m_old - m_new)
        p = jnp.exp(s - m_new)
        l_ref[...] = alpha * l_ref[...] + jnp.sum(p, axis=0, keepdims=True)
        acc_ref[...] = alpha * acc_ref[...] + _dot(vt_ref[j], p.astype(vt_ref.dtype))
        m_ref[...] = m_new

    def body(j, carry):
        block(j, False)
        return carry

    lax.fori_loop(0, i, body, 0)
    block(i, True)

    lv = lamv_ref[...]
    lam = (jnp.exp(jnp.sum(lv[0:1, :] * lv[1:2, :], axis=-1, keepdims=True))
           - jnp.exp(jnp.sum(lv[2:3, :] * lv[3:4, :], axis=-1, keepdims=True)) + lam_init)
    o = acc_ref[...] / l_ref[...]
    ot = o[:, :tq] - lam * o[:, tq:]
    ms = jnp.mean(ot * ot, axis=0, keepdims=True)
    on = ot * lax.rsqrt(ms + NORM_EPS)
    o_ref[...] = (on.T * (g_ref[...] * (1.0 - lam_init))).astype(o_ref.dtype)


def _diff_attention(lamv, qt, k, vt, subln_gain, *, tq, lam_init):
    B, d_qk, S = qt.shape
    nkv = vt.shape[1]
    tk = vt.shape[3]
    assert tq == tk
    hd2 = 2 * HEAD_DIM
    kernel = functools.partial(_attn_kernel, tq=tq, tk=tk, lam_init=lam_init)
    return pl.pallas_call(
        kernel,
        grid=(B, N_HEADS, S // tq),
        in_specs=[
            pl.BlockSpec(lamv.shape, lambda b, h, i: (0, 0)),
            pl.BlockSpec((None, hd2, tq), lambda b, h, i: (b, h, i)),
            pl.BlockSpec((None, S, hd2), lambda b, h, i: (b, 0, h)),
            pl.BlockSpec((None, nkv, V_DIM, tk), lambda b, h, i: (b, 0, h, 0)),
            pl.BlockSpec((1, V_DIM), lambda b, h, i: (0, 0)),
        ],
        out_specs=pl.BlockSpec((None, tq, V_DIM), lambda b, h, i: (b, i, h)),
        out_shape=jax.ShapeDtypeStruct((B, S, N_HEADS * V_DIM), BF16),
        scratch_shapes=[
            pltpu.VMEM((hd2, 2 * tq), BF16),
            pltpu.VMEM((1, 2 * tq), F32),
            pltpu.VMEM((1, 2 * tq), F32),
            pltpu.VMEM((V_DIM, 2 * tq), F32),
        ],
        compiler_params=_params(("arbitrary", "arbitrary", "arbitrary")),
        name="diff_attention",
    )(lamv, qt, k, vt, subln_gain)


def _post_kernel(x_ref, yr_ref, ya_ref, gm_ref, mod_ref, wpr_ref, wpa_ref, wo_ref, w1_ref, w2_ref, o_ref):
    D = x_ref.shape[-1]
    gates = jax.nn.sigmoid(gm_ref[...].astype(F32))
    merged = gates[:, :D] * _dot(yr_ref[...], wpr_ref[...]) + gates[:, D:] * _dot(ya_ref[...], wpa_ref[...])
    x1 = x_ref[...] + mod_ref[2:3, :] * _dot(merged.astype(BF16), wo_ref[...])

    h2 = _ada_norm(x1, mod_ref[3:4, :], mod_ref[4:5, :]).astype(BF16)
    d_ff = w1_ref.shape[1]
    ff = None
    for c0 in range(0, d_ff, D):
        hid = jnp.square(jnp.maximum(_dot(h2, w1_ref[:, c0:c0 + D]), 0.0)).astype(BF16)
        part = _dot(hid, w2_ref[c0:c0 + D, :])
        ff = part if ff is None else ff + part
    o_ref[...] = x1 + mod_ref[5:6, :] * ff


def _post(x, y_rnn, y_attn, gm, mod3, wpr, wpa, wo, w1, w2, *, tm):
    B, S, D = x.shape
    tok = lambda w: pl.BlockSpec((None, tm, w), lambda b, i: (b, i, 0))
    return pl.pallas_call(
        _post_kernel,
        grid=(B, S // tm),
        in_specs=[
            tok(D), tok(y_rnn.shape[-1]), tok(y_attn.shape[-1]), tok(gm.shape[-1]),
            pl.BlockSpec((None, N_MOD, D), lambda b, i: (b, 0, 0)),
            _resident(wpr.shape), _resident(wpa.shape), _resident(wo.shape),
            _resident(w1.shape), _resident(w2.shape),
        ],
        out_specs=tok(D),
        out_shape=jax.ShapeDtypeStruct((B, S, D), F32),
        compiler_params=_params(("arbitrary", "arbitrary")),
        name="post",
    )(x, y_rnn, y_attn, gm, mod3, wpr, wpa, wo, w1, w2)


def _token_tile(S, want):
    t = min(S, want)
    assert S % t == 0, (S, t)
    return t


def kernel(x, c, positions, w_ada, b_ada, w_in, conv_w, conv_b, rglru_wa, rglru_ba, rglru_wx, rglru_bx, rglru_lambda, q_norm_gain, k_norm_gain, lambda_q1, lambda_k1, lambda_q2, lambda_k2, subln_gain, w_proj_rnn, w_proj_attn, w_out, w_ff1, w_ff2):
    B, S, D = x.shape
    depth = w_in.shape[0]
    d_rnn = conv_w.shape[-1]
    d_qk = N_HEADS * 2 * HEAD_DIM
    d_attn = N_HEADS * V_DIM
    assert w_in.shape[-1] == 2 * d_rnn + 2 * d_qk + d_attn + N_BRANCH * D
    assert d_rnn == D and D % V7X_LANES == 0

    tm = _token_tile(S, 512)
    ts = _token_tile(S, 256)
    scale = HEAD_DIM ** -0.5

    cos_t, sin_t = _rope_tables(positions)
    o1, o2, o3, o4, o5 = d_rnn, 2 * d_rnn, 2 * d_rnn + d_qk, 2 * d_rnn + 2 * d_qk, 2 * d_rnn + 2 * d_qk + d_attn

    for l in range(depth):
        lam_init = 0.8 - 0.6 * math.exp(-0.3 * l)
        mod3 = _modulation(c, w_ada[l], b_ada[l]).reshape(B, N_MOD, D)

        wl = w_in[l]
        w_a = jnp.concatenate([wl[:, :o2], wl[:, o5:]], axis=1).astype(BF16)
        w_t = wl[:, o2:o5].T.astype(BF16)
        gq = jnp.broadcast_to((q_norm_gain[l] * scale)[:, None], (HEAD_DIM, tm))
        gk = jnp.broadcast_to(k_norm_gain[l][:, None], (HEAD_DIM, tm))
        xr, gr, gm, qt, k, vt = _in_projection(x, mod3, w_a, w_t, gq, gk, cos_t, sin_t, tm=tm, d_rnn=d_rnn)

        w_gate = jnp.concatenate([rglru_wa[l], rglru_wx[l]], axis=-1).astype(BF16)
        row = lambda v: v.reshape(1, -1)
        y_rnn = _rg_lru(xr, gr, conv_w[l], row(conv_b[l]), w_gate, row(rglru_ba[l]), row(rglru_bx[l]),
                        row(rglru_lambda[l]), ts=ts)

        lamv = jnp.stack([lambda_q1[l], lambda_k1[l], lambda_q2[l], lambda_k2[l]], axis=0)
        y_attn = _diff_attention(lamv, qt, k, vt, row(subln_gain[l]), tq=tm, lam_init=lam_init)

        x = _post(x, y_rnn, y_attn, gm, mod3, w_proj_rnn[l].astype(BF16), w_proj_attn[l].astype(BF16),
                  w_out[l].astype(BF16), w_ff1[l].astype(BF16), w_ff2[l].astype(BF16), tm=tm)
    return x
```

```python
import functools
import math

import jax
import jax.numpy as jnp
from jax import lax
from jax.experimental import pallas as pl
from jax.experimental.pallas import tpu as pltpu

N_HEADS = 8
HEAD_DIM = 64
V_DIM = 2 * HEAD_DIM
ROPE_THETA = 10000.0
N_RNN_BLOCKS = 8
CONV_WIDTH = 4
LRU_C = 8.0
N_BRANCH = 2
N_MOD = 6
NORM_EPS = 1e-6

V7X_SUBLANES = 8
V7X_LANES = 128
V7X_VMEM_LIMIT_BYTES = 56 * 1024 * 1024

F32 = jnp.float32
BF16 = jnp.bfloat16

_LOG2_E = math.log2(math.e)
_BOUND_SLACK = 1.0 + 2.0 ** -5
_MAX_SHIFT_SPAN_LOG2 = 96.0


def _dot(a, b):
    return jnp.dot(a, b, preferred_element_type=F32)


def _dot_nt(a, b):
    return lax.dot_general(a, b, (((1,), (1,)), ((), ())), preferred_element_type=F32)


def _resident(shape):
    zeros = (0,) * len(shape)
    return pl.BlockSpec(shape, lambda *_: zeros, pipeline_mode=pl.Buffered(1))


def _params(semantics):
    return pltpu.CompilerParams(dimension_semantics=semantics, vmem_limit_bytes=V7X_VMEM_LIMIT_BYTES)


def _mod_kernel(c_ref, w_ref, b_ref, o_ref):
    c = c_ref[...]
    ca = c * jax.nn.sigmoid(c)
    w = w_ref[...]
    ca_hi = ca.astype(BF16)
    ca_lo = (ca - ca_hi.astype(F32)).astype(BF16)
    w_hi = w.astype(BF16)
    w_lo = (w - w_hi.astype(F32)).astype(BF16)
    acc = _dot(ca_hi, w_hi) + _dot(ca_lo, w_hi) + _dot(ca_hi, w_lo)
    o_ref[...] = acc + b_ref[...]


def _modulation(c, w_ada, b_ada):
    B, D = c.shape
    n = w_ada.shape[1]
    tn = n // 4
    return pl.pallas_call(
        _mod_kernel,
        grid=(n // tn,),
        in_specs=[
            pl.BlockSpec((B, D), lambda j: (0, 0)),
            pl.BlockSpec((D, tn), lambda j: (0, j)),
            pl.BlockSpec((1, tn), lambda j: (0, j)),
        ],
        out_specs=pl.BlockSpec((B, tn), lambda j: (0, j)),
        out_shape=jax.ShapeDtypeStruct((B, n), F32),
        compiler_params=_params(("arbitrary",)),
        name="modulation",
    )(c, w_ada, b_ada.reshape(1, n))


def _rope_kernel(pos_ref, invf_ref, cos_ref, sin_ref):
    ang = invf_ref[...] * pos_ref[...].astype(F32)
    cos_ref[...] = jnp.cos(ang)
    sin_ref[...] = jnp.sin(ang)


def _rope_tables(positions):
    B, S = positions.shape
    half = HEAD_DIM // 2
    inv_freq = (ROPE_THETA ** (-jnp.arange(0, HEAD_DIM, 2, dtype=F32) / HEAD_DIM)).reshape(half, 1)
    ts = min(S, 2048)
    out = jax.ShapeDtypeStruct((B, half, S), F32)
    return pl.pallas_call(
        _rope_kernel,
        grid=(B, S // ts),
        in_specs=[
            pl.BlockSpec((None, 1, ts), lambda b, i: (b, 0, i)),
            pl.BlockSpec((half, 1), lambda b, i: (0, 0)),
        ],
        out_specs=[pl.BlockSpec((None, half, ts), lambda b, i: (b, 0, i))] * 2,
        out_shape=[out, out],
        compiler_params=_params(("arbitrary", "arbitrary")),
        name="rope_tables",
    )(positions.reshape(B, 1, S), inv_freq)


def _ada_norm(x, shift, scale):
    ms = jnp.mean(x * x, axis=-1, keepdims=True)
    return (x * lax.rsqrt(ms + NORM_EPS)) * (1.0 + scale) + shift


def _norm_rope_fm(t, gain, cos, sin):
    half = HEAD_DIM // 2
    outs = []
    for c in range(2):
        blk = t[c * HEAD_DIM:(c + 1) * HEAD_DIM, :]
        ms = jnp.mean(blk * blk, axis=0, keepdims=True)
        n = blk * lax.rsqrt(ms + NORM_EPS) * gain
        x1 = n[:half, :]
        x2 = n[half:, :]
        outs.append(x1 * cos - x2 * sin)
        outs.append(x2 * cos + x1 * sin)
    return jnp.concatenate(outs, axis=0)


def _inproj_kernel(x_ref, mod_ref, wa_ref, wt_ref, gq_ref, gk_ref, kb_ref, cos_ref, sin_ref,
                   xr_ref, gr_ref, gm_ref, qt_ref, mq_ref, k_ref, vt_ref, *, d_rnn, d_qk, d_attn):
    h = _ada_norm(x_ref[...], mod_ref[0:1, :], mod_ref[1:2, :]).astype(BF16)

    chunk = d_rnn
    col = 0
    for ref in (xr_ref, gr_ref, gm_ref):
        for c0 in range(0, ref.shape[-1], chunk):
            ref[:, c0:c0 + chunk] = _dot(h, wa_ref[:, col:col + chunk]).astype(ref.dtype)
            col += chunk

    cos = cos_ref[...]
    sin = sin_ref[...]
    hd2 = 2 * HEAD_DIM
    for hh in range(N_HEADS):
        qh = _dot_nt(wt_ref[hh * hd2:(hh + 1) * hd2, :], h)
        qb = _norm_rope_fm(qh, gq_ref[...], cos, sin).astype(qt_ref.dtype)
        qt_ref[hh * hd2:(hh + 1) * hd2, :] = qb
        q2 = jnp.square(qb.astype(F32))
        for c in range(2):
            qn = jnp.sqrt(jnp.sum(q2[c * HEAD_DIM:(c + 1) * HEAD_DIM, :], axis=0, keepdims=True))
            mq_ref[hh, c:c + 1, :] = qn * kb_ref[...]
    for hh in range(N_HEADS):
        kh = _dot_nt(wt_ref[d_qk + hh * hd2:d_qk + (hh + 1) * hd2, :], h)
        kh = _norm_rope_fm(kh, gk_ref[...], cos, sin)
        k_ref[:, hh * hd2:(hh + 1) * hd2] = kh.T.astype(k_ref.dtype)
    vt_ref[...] = _dot_nt(wt_ref[2 * d_qk:2 * d_qk + d_attn, :], h).astype(vt_ref.dtype)


def _in_projection(x, mod3, w_a, w_t, gq, gk, kb, cos_t, sin_t, *, tm, d_rnn):
    B, S, D = x.shape
    d_qk = N_HEADS * 2 * HEAD_DIM
    d_attn = N_HEADS * V_DIM
    d_gm = N_BRANCH * D
    half = HEAD_DIM // 2
    nt = S // tm
    tok = lambda w: pl.BlockSpec((None, tm, w), lambda b, i: (b, i, 0))
    fm = lambda r: pl.BlockSpec((None, r, tm), lambda b, i: (b, 0, i))
    kernel = functools.partial(_inproj_kernel, d_rnn=d_rnn, d_qk=d_qk, d_attn=d_attn)
    return pl.pallas_call(
        kernel,
        grid=(B, nt),
        in_specs=[
            tok(D),
            pl.BlockSpec((None, N_MOD, D), lambda b, i: (b, 0, 0)),
            _resident(w_a.shape),
            _resident(w_t.shape),
            _resident(gq.shape),
            _resident(gk.shape),
            _resident(kb.shape),
            fm(half),
            fm(half),
        ],
        out_specs=[
            tok(d_rnn), tok(d_rnn), tok(d_gm),
            fm(d_qk),
            pl.BlockSpec((None, N_HEADS, 2, tm), lambda b, i: (b, 0, 0, i)),
            tok(d_qk),
            pl.BlockSpec((None, None, d_attn, tm), lambda b, i: (b, i, 0, 0)),
        ],
        out_shape=[
            jax.ShapeDtypeStruct((B, S, d_rnn), BF16),
            jax.ShapeDtypeStruct((B, S, d_rnn), BF16),
            jax.ShapeDtypeStruct((B, S, d_gm), BF16),
            jax.ShapeDtypeStruct((B, d_qk, S), BF16),
            jax.ShapeDtypeStruct((B, N_HEADS, 2, S), F32),
            jax.ShapeDtypeStruct((B, S, d_qk), BF16),
            jax.ShapeDtypeStruct((B, nt, d_attn, tm), BF16),
        ],
        compiler_params=_params(("arbitrary", "arbitrary")),
        name="in_projection",
    )(x, mod3, w_a, w_t, gq, gk, kb, cos_t, sin_t)


def _shift_rows(x, prev8, k, row8):
    rolled = pltpu.roll(x, k, 0)
    top = jnp.where(row8 < k, pltpu.roll(prev8, k, 0), rolled[:V7X_SUBLANES, :])
    return jnp.concatenate([top, rolled[V7X_SUBLANES:, :]], axis=0)


def _rglru_kernel(xr_ref, gr_ref, cw_ref, cb_ref, wg_ref, ba_ref, bx_ref, lam_ref,
                  y_ref, tail_ref, h_ref):
    ts, C = xr_ref.shape
    blk = C // N_RNN_BLOCKS

    @pl.when(pl.program_id(1) == 0)
    def _():
        tail_ref[...] = jnp.zeros_like(tail_ref)
        h_ref[...] = jnp.zeros_like(h_ref)

    x = xr_ref[...].astype(F32)
    prev8 = tail_ref[...]
    row8 = lax.broadcasted_iota(jnp.int32, (V7X_SUBLANES, C), 0)
    conv = x * cw_ref[CONV_WIDTH - 1:CONV_WIDTH, :] + cb_ref[...]
    for d in range(1, CONV_WIDTH):
        conv = conv + _shift_rows(x, prev8, d, row8) * cw_ref[CONV_WIDTH - 1 - d:CONV_WIDTH - d, :]
    tail_ref[...] = x[ts - V7X_SUBLANES:, :]

    convb = conv.astype(BF16)
    pre = [_dot(convb[:, n * blk:(n + 1) * blk], wg_ref[n]) for n in range(N_RNN_BLOCKS)]
    r = jax.nn.sigmoid(jnp.concatenate([p[:, :blk] for p in pre], axis=1) + ba_ref[...])
    i = jax.nn.sigmoid(jnp.concatenate([p[:, blk:] for p in pre], axis=1) + bx_ref[...])

    z = -lam_ref[...]
    softplus = jnp.maximum(z, 0.0) + jnp.log1p(jnp.exp(-jnp.abs(z)))
    log_a = (-LRU_C) * r * softplus
    a = jnp.exp(log_a)
    u = jnp.sqrt(-jnp.tanh(log_a) * (a * a + 1.0)) * (i * conv)

    rowg = lax.broadcasted_iota(jnp.int32, (ts, C), 0) % V7X_SUBLANES
    s = 1
    while s < V7X_SUBLANES:
        ok = rowg >= s
        u = jnp.where(ok, u + a * pltpu.roll(u, s, 0), u)
        a = jnp.where(ok, a * pltpu.roll(a, s, 0), a)
        s *= 2
    hprev = h_ref[...]
    hs = []
    for g in range(ts // V7X_SUBLANES):
        sl = slice(g * V7X_SUBLANES, (g + 1) * V7X_SUBLANES)
        hg = u[sl, :] + a[sl, :] * hprev
        hs.append(hg)
        hprev = jnp.broadcast_to(hg[V7X_SUBLANES - 1:, :], hg.shape)
    h_ref[...] = hprev
    h = jnp.concatenate(hs, axis=0)

    y_ref[...] = (h * jax.nn.gelu(gr_ref[...].astype(F32))).astype(y_ref.dtype)


def _rg_lru(xr, gr, conv_w, conv_b, w_gate, b_a, b_x, lam, *, ts):
    B, S, C = xr.shape
    tok = pl.BlockSpec((None, ts, C), lambda b, i: (b, i, 0))
    row = lambda a: _resident(a.shape)
    return pl.pallas_call(
        _rglru_kernel,
        grid=(B, S // ts),
        in_specs=[tok, tok, row(conv_w), row(conv_b), row(w_gate), row(b_a), row(b_x), row(lam)],
        out_specs=tok,
        out_shape=jax.ShapeDtypeStruct((B, S, C), BF16),
        scratch_shapes=[pltpu.VMEM((V7X_SUBLANES, C), F32), pltpu.VMEM((V7X_SUBLANES, C), F32)],
        compiler_params=_params(("arbitrary", "arbitrary")),
        name="rg_lru",
    )(xr, gr, conv_w, conv_b, w_gate, b_a, b_x, lam)


def _attn_kernel(lamv_ref, qt_ref, mq_ref, k_ref, vt_ref, g_ref, o_ref, qbd_ref, m_ref, l_ref, acc_ref,
                 *, tq, tk, lam_init, online):
    i = pl.program_id(2)

    zero = jnp.zeros((HEAD_DIM, tq), qbd_ref.dtype)
    qbd_ref[:HEAD_DIM, :tq] = qt_ref[:HEAD_DIM, :]
    qbd_ref[:HEAD_DIM, tq:] = zero
    qbd_ref[HEAD_DIM:, :tq] = zero
    qbd_ref[HEAD_DIM:, tq:] = qt_ref[HEAD_DIM:, :]
    if online:
        m_ref[...] = jnp.full(m_ref.shape, -jnp.inf, F32)
    else:
        m_ref[...] = jnp.concatenate([mq_ref[0:1, :], mq_ref[1:2, :]], axis=1)
    l_ref[...] = jnp.zeros_like(l_ref)
    acc_ref[...] = jnp.zeros_like(acc_ref)

    def block(j, masked):
        start = pl.multiple_of(j * tk, tk)
        s = _dot(k_ref[pl.ds(start, tk), :], qbd_ref[...])
        if masked:
            key = lax.broadcasted_iota(jnp.int32, s.shape, 0)
            col = lax.broadcasted_iota(jnp.int32, s.shape, 1)
            qry = jnp.where(col >= tq, col - tq, col)
            s = jnp.where(key <= qry, s, -jnp.inf)
        if online:
            m_old = m_ref[...]
            m_new = jnp.maximum(m_old, jnp.max(s, axis=0, keepdims=True))
            alpha = jnp.exp2(m_old - m_new)
            p = jnp.exp2(s - m_new)
            psum = jnp.sum(p.reshape(tk // V7X_SUBLANES, V7X_SUBLANES, 2 * tq), axis=0)
            l_ref[...] = alpha * l_ref[...] + psum
            acc_ref[...] = alpha * acc_ref[...] + _dot(vt_ref[j], p.astype(vt_ref.dtype))
            m_ref[...] = m_new
        else:
            p = jnp.exp2(s - m_ref[...])
            l_ref[...] += jnp.sum(p.reshape(tk // V7X_SUBLANES, V7X_SUBLANES, 2 * tq), axis=0)
            acc_ref[...] += _dot(vt_ref[j], p.astype(vt_ref.dtype))

    def body(j, carry):
        block(j, False)
        return carry

    lax.fori_loop(0, i, body, 0)
    block(i, True)

    lv = lamv_ref[...]
    lam = (jnp.exp(jnp.sum(lv[0:1, :] * lv[1:2, :], axis=-1, keepdims=True))
           - jnp.exp(jnp.sum(lv[2:3, :] * lv[3:4, :], axis=-1, keepdims=True)) + lam_init)
    o = acc_ref[...] / jnp.sum(l_ref[...], axis=0, keepdims=True)
    ot = o[:, :tq] - lam * o[:, tq:]
    ms = jnp.mean(ot * ot, axis=0, keepdims=True)
    on = ot * lax.rsqrt(ms + NORM_EPS)
    o_ref[...] = (on.T * (g_ref[...] * (1.0 - lam_init))).astype(o_ref.dtype)


def _diff_attention(lamv, qt, mq, k, vt, subln_gain, *, tq, lam_init, online):
    B, d_qk, S = qt.shape
    nkv = vt.shape[1]
    tk = vt.shape[3]
    assert tq == tk
    hd2 = 2 * HEAD_DIM
    kernel = functools.partial(_attn_kernel, tq=tq, tk=tk, lam_init=lam_init, online=online)
    return pl.pallas_call(
        kernel,
        grid=(B, N_HEADS, S // tq),
        in_specs=[
            pl.BlockSpec(lamv.shape, lambda b, h, i: (0, 0)),
            pl.BlockSpec((None, hd2, tq), lambda b, h, i: (b, h, i)),
            pl.BlockSpec((None, None, 2, tq), lambda b, h, i: (b, h, 0, i)),
            pl.BlockSpec((None, S, hd2), lambda b, h, i: (b, 0, h)),
            pl.BlockSpec((None, nkv, V_DIM, tk), lambda b, h, i: (b, 0, h, 0)),
            pl.BlockSpec((1, V_DIM), lambda b, h, i: (0, 0)),
        ],
        out_specs=pl.BlockSpec((None, tq, V_DIM), lambda b, h, i: (b, i, h)),
        out_shape=jax.ShapeDtypeStruct((B, S, N_HEADS * V_DIM), BF16),
        scratch_shapes=[
            pltpu.VMEM((hd2, 2 * tq), BF16),
            pltpu.VMEM((1, 2 * tq), F32),
            pltpu.VMEM((V7X_SUBLANES, 2 * tq), F32),
            pltpu.VMEM((V_DIM, 2 * tq), F32),
        ],
        compiler_params=_params(("arbitrary", "arbitrary", "arbitrary")),
        name="diff_attention_online" if online else "diff_attention",
    )(lamv, qt, mq, k, vt, subln_gain)


def _post_kernel(x_ref, yr_ref, ya_ref, gm_ref, mod_ref, wpr_ref, wpa_ref, wo_ref, w1_ref, w2_ref, o_ref):
    D = x_ref.shape[-1]
    gates = jax.nn.sigmoid(gm_ref[...].astype(F32))
    merged = gates[:, :D] * _dot(yr_ref[...], wpr_ref[...]) + gates[:, D:] * _dot(ya_ref[...], wpa_ref[...])
    x1 = x_ref[...] + mod_ref[2:3, :] * _dot(merged.astype(BF16), wo_ref[...])

    h2 = _ada_norm(x1, mod_ref[3:4, :], mod_ref[4:5, :]).astype(BF16)
    d_ff = w1_ref.shape[1]
    ff = None
    for c0 in range(0, d_ff, D):
        hid = jnp.square(jnp.maximum(_dot(h2, w1_ref[:, c0:c0 + D]), 0.0)).astype(BF16)
        part = _dot(hid, w2_ref[c0:c0 + D, :])
        ff = part if ff is None else ff + part
    o_ref[...] = x1 + mod_ref[5:6, :] * ff


def _post(x, y_rnn, y_attn, gm, mod3, wpr, wpa, wo, w1, w2, *, tm):
    B, S, D = x.shape
    tok = lambda w: pl.BlockSpec((None, tm, w), lambda b, i: (b, i, 0))
    return pl.pallas_call(
        _post_kernel,
        grid=(B, S // tm),
        in_specs=[
            tok(D), tok(y_rnn.shape[-1]), tok(y_attn.shape[-1]), tok(gm.shape[-1]),
            pl.BlockSpec((None, N_MOD, D), lambda b, i: (b, 0, 0)),
            _resident(wpr.shape), _resident(wpa.shape), _resident(wo.shape),
            _resident(w1.shape), _resident(w2.shape),
        ],
        out_specs=tok(D),
        out_shape=jax.ShapeDtypeStruct((B, S, D), F32),
        compiler_params=_params(("arbitrary", "arbitrary")),
        name="post",
    )(x, y_rnn, y_attn, gm, mod3, wpr, wpa, wo, w1, w2)


def _token_tile(S, want):
    t = min(S, want)
    assert S % t == 0, (S, t)
    return t


def kernel(x, c, positions, w_ada, b_ada, w_in, conv_w, conv_b, rglru_wa, rglru_ba, rglru_wx, rglru_bx, rglru_lambda, q_norm_gain, k_norm_gain, lambda_q1, lambda_k1, lambda_q2, lambda_k2, subln_gain, w_proj_rnn, w_proj_attn, w_out, w_ff1, w_ff2):
    B, S, D = x.shape
    depth = w_in.shape[0]
    d_rnn = conv_w.shape[-1]
    d_qk = N_HEADS * 2 * HEAD_DIM
    d_attn = N_HEADS * V_DIM
    assert w_in.shape[-1] == 2 * d_rnn + 2 * d_qk + d_attn + N_BRANCH * D
    assert d_rnn == D and D % V7X_LANES == 0

    tm = _token_tile(S, 512)
    ts = _token_tile(S, 256)
    scale = HEAD_DIM ** -0.5

    cos_t, sin_t = _rope_tables(positions)
    o1, o2, o3, o4, o5 = d_rnn, 2 * d_rnn, 2 * d_rnn + d_qk, 2 * d_rnn + 2 * d_qk, 2 * d_rnn + 2 * d_qk + d_attn

    for l in range(depth):
        lam_init = 0.8 - 0.6 * math.exp(-0.3 * l)
        mod3 = _modulation(c, w_ada[l], b_ada[l]).reshape(B, N_MOD, D)

        wl = w_in[l]
        w_a = jnp.concatenate([wl[:, :o2], wl[:, o5:]], axis=1).astype(BF16)
        w_t = wl[:, o2:o5].T.astype(BF16)
        gq = jnp.broadcast_to((q_norm_gain[l] * (scale * _LOG2_E))[:, None], (HEAD_DIM, tm))
        gk = jnp.broadcast_to(k_norm_gain[l][:, None], (HEAD_DIM, tm))
        k_bound = math.sqrt(HEAD_DIM) * jnp.max(jnp.abs(k_norm_gain[l])) * _BOUND_SLACK
        q_bound = math.sqrt(HEAD_DIM) * jnp.max(jnp.abs(q_norm_gain[l])) * (scale * _LOG2_E)
        kb = jnp.broadcast_to(k_bound.reshape(1, 1), (1, tm))
        xr, gr, gm, qt, mq, k, vt = _in_projection(x, mod3, w_a, w_t, gq, gk, kb, cos_t, sin_t, tm=tm, d_rnn=d_rnn)

        w_gate = jnp.concatenate([rglru_wa[l], rglru_wx[l]], axis=-1).astype(BF16)
        row = lambda v: v.reshape(1, -1)
        y_rnn = _rg_lru(xr, gr, conv_w[l], row(conv_b[l]), w_gate, row(rglru_ba[l]), row(rglru_bx[l]),
                        row(rglru_lambda[l]), ts=ts)

        lamv = jnp.stack([lambda_q1[l], lambda_k1[l], lambda_q2[l], lambda_k2[l]], axis=0)
        attn = functools.partial(_diff_attention, tq=tm, lam_init=lam_init)
        y_attn = lax.cond(
            2.0 * q_bound * k_bound <= _MAX_SHIFT_SPAN_LOG2,
            functools.partial(attn, online=False), functools.partial(attn, online=True),
            lamv, qt, mq, k, vt, row(subln_gain[l]))

        x = _post(x, y_rnn, y_attn, gm, mod3, w_proj_rnn[l].astype(BF16), w_proj_attn[l].astype(BF16),
                  w_out[l].astype(BF16), w_ff1[l].astype(BF16), w_ff2[l].astype(BF16), tm=tm)
    return x
```

```python
import functools
import math

import jax
import jax.numpy as jnp
from jax import lax
from jax.experimental import pallas as pl
from jax.experimental.pallas import tpu as pltpu

N_HEADS = 8
HEAD_DIM = 64
V_DIM = 2 * HEAD_DIM
ROPE_THETA = 10000.0
N_RNN_BLOCKS = 8
CONV_WIDTH = 4
LRU_C = 8.0
N_BRANCH = 2
N_MOD = 6
NORM_EPS = 1e-6

V7X_SUBLANES = 8
V7X_LANES = 128
V7X_VMEM_LIMIT_BYTES = 56 * 1024 * 1024

F32 = jnp.float32
BF16 = jnp.bfloat16

_LOG2_E = math.log2(math.e)
_BOUND_SLACK = 1.0 + 2.0 ** -6
_MAX_ABS_SCORE_LOG2 = 48.0


def _dot(a, b):
    return jnp.dot(a, b, preferred_element_type=F32)


def _dot_nt(a, b):
    return lax.dot_general(a, b, (((1,), (1,)), ((), ())), preferred_element_type=F32)


def _resident(shape):
    zeros = (0,) * len(shape)
    return pl.BlockSpec(shape, lambda *_: zeros, pipeline_mode=pl.Buffered(1))


def _params(semantics):
    return pltpu.CompilerParams(dimension_semantics=semantics, vmem_limit_bytes=V7X_VMEM_LIMIT_BYTES)


def _mod_kernel(c_ref, w_ref, b_ref, o_ref):
    c = c_ref[...]
    ca = c * jax.nn.sigmoid(c)
    w = w_ref[...]
    ca_hi = ca.astype(BF16)
    ca_lo = (ca - ca_hi.astype(F32)).astype(BF16)
    w_hi = w.astype(BF16)
    w_lo = (w - w_hi.astype(F32)).astype(BF16)
    acc = _dot(ca_hi, w_hi) + _dot(ca_lo, w_hi) + _dot(ca_hi, w_lo)
    o_ref[...] = acc + b_ref[...]


def _modulation(c, w_ada, b_ada):
    B, D = c.shape
    n = w_ada.shape[1]
    tn = n // 4
    return pl.pallas_call(
        _mod_kernel,
        grid=(n // tn,),
        in_specs=[
            pl.BlockSpec((B, D), lambda j: (0, 0)),
            pl.BlockSpec((D, tn), lambda j: (0, j)),
            pl.BlockSpec((1, tn), lambda j: (0, j)),
        ],
        out_specs=pl.BlockSpec((B, tn), lambda j: (0, j)),
        out_shape=jax.ShapeDtypeStruct((B, n), F32),
        compiler_params=_params(("arbitrary",)),
        name="modulation",
    )(c, w_ada, b_ada.reshape(1, n))


def _rope_kernel(pos_ref, invf_ref, cos_ref, sin_ref):
    ang = invf_ref[...] * pos_ref[...].astype(F32)
    cos_ref[...] = jnp.cos(ang)
    sin_ref[...] = jnp.sin(ang)


def _rope_tables(positions):
    B, S = positions.shape
    half = HEAD_DIM // 2
    inv_freq = (ROPE_THETA ** (-jnp.arange(0, HEAD_DIM, 2, dtype=F32) / HEAD_DIM)).reshape(half, 1)
    ts = S // 4 if S % (4 * V7X_LANES) == 0 else S
    out = jax.ShapeDtypeStruct((B, half, S), F32)
    return pl.pallas_call(
        _rope_kernel,
        grid=(B, S // ts),
        in_specs=[
            pl.BlockSpec((None, 1, ts), lambda b, i: (b, 0, i)),
            pl.BlockSpec((half, 1), lambda b, i: (0, 0)),
        ],
        out_specs=[pl.BlockSpec((None, half, ts), lambda b, i: (b, 0, i))] * 2,
        out_shape=[out, out],
        compiler_params=_params(("arbitrary", "arbitrary")),
        name="rope_tables",
    )(positions.reshape(B, 1, S), inv_freq)


def _ada_norm(x, shift, scale):
    ms = jnp.mean(x * x, axis=-1, keepdims=True)
    return (x * lax.rsqrt(ms + NORM_EPS)) * (1.0 + scale) + shift


def _norm_rope_fm(t, gain, cos, sin):
    half = HEAD_DIM // 2
    outs = []
    for c in range(2):
        blk = t[c * HEAD_DIM:(c + 1) * HEAD_DIM, :]
        ms = jnp.mean(blk * blk, axis=0, keepdims=True)
        n = blk * lax.rsqrt(ms + NORM_EPS) * gain
        x1 = n[:half, :]
        x2 = n[half:, :]
        outs.append(x1 * cos - x2 * sin)
        outs.append(x2 * cos + x1 * sin)
    return jnp.concatenate(outs, axis=0)


def _inproj_kernel(x_ref, mod_ref, wa_ref, wt_ref, gq_ref, gk_ref, cos_ref, sin_ref,
                   xr_ref, gr_ref, gm_ref, qt_ref, k_ref, vt_ref, *, d_rnn, d_qk, d_attn):
    h = _ada_norm(x_ref[...], mod_ref[0:1, :], mod_ref[1:2, :]).astype(BF16)

    def token_major(ref, col):
        for c0 in range(0, ref.shape[-1], d_rnn):
            ref[:, c0:c0 + d_rnn] = _dot(h, wa_ref[:, col + c0:col + c0 + d_rnn]).astype(ref.dtype)

    token_major(xr_ref, 0)
    token_major(gr_ref, d_rnn)
    q_raw = _dot_nt(wt_ref[0:d_qk, :], h)
    k_raw = _dot_nt(wt_ref[d_qk:2 * d_qk, :], h)
    token_major(gm_ref, 2 * d_rnn)
    vt_ref[...] = _dot_nt(wt_ref[2 * d_qk:2 * d_qk + d_attn, :], h).astype(vt_ref.dtype)

    cos = cos_ref[...]
    sin = sin_ref[...]
    hd2 = 2 * HEAD_DIM
    for hh in range(N_HEADS):
        rows = slice(hh * hd2, (hh + 1) * hd2)
        qt_ref[rows, :] = _norm_rope_fm(q_raw[rows, :], gq_ref[...], cos, sin).astype(qt_ref.dtype)
        kh = _norm_rope_fm(k_raw[rows, :], gk_ref[...], cos, sin)
        k_ref[:, rows] = kh.T.astype(k_ref.dtype)


def _in_projection(x, mod3, w_a, w_t, gq, gk, cos_t, sin_t, *, tm, d_rnn):
    B, S, D = x.shape
    d_qk = N_HEADS * 2 * HEAD_DIM
    d_attn = N_HEADS * V_DIM
    d_gm = N_BRANCH * D
    half = HEAD_DIM // 2
    nt = S // tm
    tok = lambda w: pl.BlockSpec((None, tm, w), lambda b, i: (b, i, 0))
    fm = lambda r: pl.BlockSpec((None, r, tm), lambda b, i: (b, 0, i))
    kernel = functools.partial(_inproj_kernel, d_rnn=d_rnn, d_qk=d_qk, d_attn=d_attn)
    return pl.pallas_call(
        kernel,
        grid=(B, nt),
        in_specs=[
            tok(D),
            pl.BlockSpec((None, N_MOD, D), lambda b, i: (b, 0, 0)),
            _resident(w_a.shape),
            _resident(w_t.shape),
            _resident(gq.shape),
            _resident(gk.shape),
            fm(half),
            fm(half),
        ],
        out_specs=[
            tok(d_rnn), tok(d_rnn), tok(d_gm),
            fm(d_qk),
            tok(d_qk),
            pl.BlockSpec((None, None, d_attn, tm), lambda b, i: (b, i, 0, 0)),
        ],
        out_shape=[
            jax.ShapeDtypeStruct((B, S, d_rnn), BF16),
            jax.ShapeDtypeStruct((B, S, d_rnn), BF16),
            jax.ShapeDtypeStruct((B, S, d_gm), BF16),
            jax.ShapeDtypeStruct((B, d_qk, S), BF16),
            jax.ShapeDtypeStruct((B, S, d_qk), BF16),
            jax.ShapeDtypeStruct((B, nt, d_attn, tm), BF16),
        ],
        compiler_params=_params(("arbitrary", "arbitrary")),
        name="in_projection",
    )(x, mod3, w_a, w_t, gq, gk, cos_t, sin_t)


def _shift_rows(x, prev8, k, row8):
    rolled = pltpu.roll(x, k, 0)
    top = jnp.where(row8 < k, pltpu.roll(prev8, k, 0), rolled[:V7X_SUBLANES, :])
    return jnp.concatenate([top, rolled[V7X_SUBLANES:, :]], axis=0)


def _rglru_kernel(xr_ref, gr_ref, cw_ref, cb_ref, wg_ref, ba_ref, bx_ref, lam_ref,
                  y_ref, tail_ref, h_ref):
    ts, C = xr_ref.shape
    blk = C // N_RNN_BLOCKS

    @pl.when(pl.program_id(1) == 0)
    def _():
        tail_ref[...] = jnp.zeros_like(tail_ref)
        h_ref[...] = jnp.zeros_like(h_ref)

    x = xr_ref[...].astype(F32)
    prev8 = tail_ref[...]
    row8 = lax.broadcasted_iota(jnp.int32, (V7X_SUBLANES, C), 0)
    conv = x * cw_ref[CONV_WIDTH - 1:CONV_WIDTH, :] + cb_ref[...]
    for d in range(1, CONV_WIDTH):
        conv = conv + _shift_rows(x, prev8, d, row8) * cw_ref[CONV_WIDTH - 1 - d:CONV_WIDTH - d, :]
    tail_ref[...] = x[ts - V7X_SUBLANES:, :]

    convb = conv.astype(BF16)
    pre = [_dot(convb[:, n * blk:(n + 1) * blk], wg_ref[n]) for n in range(N_RNN_BLOCKS)]
    r = jax.nn.sigmoid(jnp.concatenate([p[:, :blk] for p in pre], axis=1) + ba_ref[...])
    i = jax.nn.sigmoid(jnp.concatenate([p[:, blk:] for p in pre], axis=1) + bx_ref[...])

    z = -lam_ref[...]
    softplus = jnp.maximum(z, 0.0) + jnp.log1p(jnp.exp(-jnp.abs(z)))
    log_a = (-LRU_C) * r * softplus
    a = jnp.exp(log_a)
    u = jnp.sqrt(-jnp.tanh(log_a) * (a * a + 1.0)) * (i * conv)

    rowg = lax.broadcasted_iota(jnp.int32, (ts, C), 0) % V7X_SUBLANES
    s = 1
    while s < V7X_SUBLANES:
        ok = rowg >= s
        u = jnp.where(ok, u + a * pltpu.roll(u, s, 0), u)
        a = jnp.where(ok, a * pltpu.roll(a, s, 0), a)
        s *= 2
    hprev = h_ref[...]
    hs = []
    for g in range(ts // V7X_SUBLANES):
        sl = slice(g * V7X_SUBLANES, (g + 1) * V7X_SUBLANES)
        hg = u[sl, :] + a[sl, :] * hprev
        hs.append(hg)
        hprev = jnp.broadcast_to(hg[V7X_SUBLANES - 1:, :], hg.shape)
    h_ref[...] = hprev
    h = jnp.concatenate(hs, axis=0)

    y_ref[...] = (h * jax.nn.gelu(gr_ref[...].astype(F32))).astype(y_ref.dtype)


def _rg_lru(xr, gr, conv_w, conv_b, w_gate, b_a, b_x, lam, *, ts):
    B, S, C = xr.shape
    tok = pl.BlockSpec((None, ts, C), lambda b, i: (b, i, 0))
    row = lambda a: _resident(a.shape)
    return pl.pallas_call(
        _rglru_kernel,
        grid=(B, S // ts),
        in_specs=[tok, tok, row(conv_w), row(conv_b), row(w_gate), row(b_a), row(b_x), row(lam)],
        out_specs=tok,
        out_shape=jax.ShapeDtypeStruct((B, S, C), BF16),
        scratch_shapes=[pltpu.VMEM((V7X_SUBLANES, C), F32), pltpu.VMEM((V7X_SUBLANES, C), F32)],
        compiler_params=_params(("arbitrary", "arbitrary")),
        name="rg_lru",
    )(xr, gr, conv_w, conv_b, w_gate, b_a, b_x, lam)


_KV_UNROLL = 4


def _attn_kernel(lamv_ref, qt_ref, k_ref, vt_ref, g_ref, o_ref, qbd_ref, m_ref, l_ref, acc_ref,
                 *, tq, tk, lam_init, online):
    i = pl.program_id(2)

    zero = jnp.zeros((HEAD_DIM, tq), qbd_ref.dtype)
    qbd_ref[:HEAD_DIM, :tq] = qt_ref[:HEAD_DIM, :]
    qbd_ref[:HEAD_DIM, tq:] = zero
    qbd_ref[HEAD_DIM:, :tq] = zero
    qbd_ref[HEAD_DIM:, tq:] = qt_ref[HEAD_DIM:, :]
    if online:
        m_ref[...] = jnp.full(m_ref.shape, -jnp.inf, F32)
    l_ref[...] = jnp.zeros_like(l_ref)
    acc_ref[...] = jnp.zeros_like(acc_ref)

    def scores(j, masked):
        start = pl.multiple_of(j * tk, tk)
        s = _dot(k_ref[pl.ds(start, tk), :], qbd_ref[...])
        if masked:
            key = lax.broadcasted_iota(jnp.int32, s.shape, 0)
            col = lax.broadcasted_iota(jnp.int32, s.shape, 1)
            qry = jnp.where(col >= tq, col - tq, col)
            s = jnp.where(key <= qry, s, -jnp.inf)
        return s

    def colsum8(p):
        return jnp.sum(p.reshape(tk // V7X_SUBLANES, V7X_SUBLANES, 2 * tq), axis=0)

    def online_block(j, masked):
        s = scores(j, masked)
        m_old = m_ref[...]
        m_new = jnp.maximum(m_old, jnp.max(s, axis=0, keepdims=True))
        alpha = jnp.exp2(m_old - m_new)
        p = jnp.exp2(s - m_new)
        l_ref[...] = alpha * l_ref[...] + colsum8(p)
        acc_ref[...] = alpha * acc_ref[...] + _dot(vt_ref[j], p.astype(vt_ref.dtype))
        m_ref[...] = m_new

    def streaming_blocks(base, count, mask_last):
        lsum = pv = None
        for u in range(count):
            p = jnp.exp2(scores(base + u, mask_last and u == count - 1))
            ps = colsum8(p)
            d = _dot(vt_ref[base + u], p.astype(vt_ref.dtype))
            lsum = ps if lsum is None else lsum + ps
            pv = d if pv is None else pv + d
        l_ref[...] += lsum
        acc_ref[...] += pv

    if online:
        def body(j, carry):
            online_block(j, False)
            return carry

        lax.fori_loop(0, i, body, 0)
        online_block(i, True)
    else:
        def body(t, carry):
            streaming_blocks(t * _KV_UNROLL, _KV_UNROLL, False)
            return carry

        rem = lax.rem(i, _KV_UNROLL)
        lax.fori_loop(0, lax.div(i, _KV_UNROLL), body, 0)
        for r in range(_KV_UNROLL):
            @pl.when(rem == r)
            def _(r=r):
                streaming_blocks(i - r, r + 1, True)

    lv = lamv_ref[...]
    lam = (jnp.exp(jnp.sum(lv[0:1, :] * lv[1:2, :], axis=-1, keepdims=True))
           - jnp.exp(jnp.sum(lv[2:3, :] * lv[3:4, :], axis=-1, keepdims=True)) + lam_init)
    o = acc_ref[...] / jnp.sum(l_ref[...], axis=0, keepdims=True)
    ot = o[:, :tq] - lam * o[:, tq:]
    ms = jnp.mean(ot * ot, axis=0, keepdims=True)
    on = ot * lax.rsqrt(ms + NORM_EPS)
    o_ref[...] = (on.T * (g_ref[...] * (1.0 - lam_init))).astype(o_ref.dtype)


def _diff_attention(lamv, qt, k, vt, subln_gain, *, tq, lam_init, online):
    B, d_qk, S = qt.shape
    nkv = vt.shape[1]
    tk = vt.shape[3]
    assert tq == tk
    hd2 = 2 * HEAD_DIM
    kernel = functools.partial(_attn_kernel, tq=tq, tk=tk, lam_init=lam_init, online=online)
    return pl.pallas_call(
        kernel,
        grid=(B, N_HEADS, S // tq),
        in_specs=[
            pl.BlockSpec(lamv.shape, lambda b, h, i: (0, 0)),
            pl.BlockSpec((None, hd2, tq), lambda b, h, i: (b, h, i)),
            pl.BlockSpec((None, S, hd2), lambda b, h, i: (b, 0, h)),
            pl.BlockSpec((None, nkv, V_DIM, tk), lambda b, h, i: (b, 0, h, 0)),
            pl.BlockSpec((1, V_DIM), lambda b, h, i: (0, 0)),
        ],
        out_specs=pl.BlockSpec((None, tq, V_DIM), lambda b, h, i: (b, i, h)),
        out_shape=jax.ShapeDtypeStruct((B, S, N_HEADS * V_DIM), BF16),
        scratch_shapes=[
            pltpu.VMEM((hd2, 2 * tq), BF16),
            pltpu.VMEM((1, 2 * tq), F32),
            pltpu.VMEM((V7X_SUBLANES, 2 * tq), F32),
            pltpu.VMEM((V_DIM, 2 * tq), F32),
        ],
        compiler_params=_params(("arbitrary", "arbitrary", "arbitrary")),
        name="diff_attention_online" if online else "diff_attention",
    )(lamv, qt, k, vt, subln_gain)


def _post_kernel(x_ref, yr_ref, ya_ref, gm_ref, mod_ref, wpr_ref, wpa_ref, wo_ref, w1_ref, w2_ref, o_ref):
    D = x_ref.shape[-1]
    gates = jax.nn.sigmoid(gm_ref[...].astype(F32))
    merged = gates[:, :D] * _dot(yr_ref[...], wpr_ref[...]) + gates[:, D:] * _dot(ya_ref[...], wpa_ref[...])
    x1 = x_ref[...] + mod_ref[2:3, :] * _dot(merged.astype(BF16), wo_ref[...])

    h2 = _ada_norm(x1, mod_ref[3:4, :], mod_ref[4:5, :]).astype(BF16)
    d_ff = w1_ref.shape[1]
    ff = None
    for c0 in range(0, d_ff, D):
        hid = jnp.square(jnp.maximum(_dot(h2, w1_ref[:, c0:c0 + D]), 0.0)).astype(BF16)
        part = _dot(hid, w2_ref[c0:c0 + D, :])
        ff = part if ff is None else ff + part
    o_ref[...] = x1 + mod_ref[5:6, :] * ff


def _post(x, y_rnn, y_attn, gm, mod3, wpr, wpa, wo, w1, w2, *, tm):
    B, S, D = x.shape
    tok = lambda w: pl.BlockSpec((None, tm, w), lambda b, i: (b, i, 0))
    return pl.pallas_call(
        _post_kernel,
        grid=(B, S // tm),
        in_specs=[
            tok(D), tok(y_rnn.shape[-1]), tok(y_attn.shape[-1]), tok(gm.shape[-1]),
            pl.BlockSpec((None, N_MOD, D), lambda b, i: (b, 0, 0)),
            _resident(wpr.shape), _resident(wpa.shape), _resident(wo.shape),
            _resident(w1.shape), _resident(w2.shape),
        ],
        out_specs=tok(D),
        out_shape=jax.ShapeDtypeStruct((B, S, D), F32),
        compiler_params=_params(("arbitrary", "arbitrary")),
        name="post",
    )(x, y_rnn, y_attn, gm, mod3, wpr, wpa, wo, w1, w2)


def _token_tile(S, want):
    t = min(S, want)
    assert S % t == 0, (S, t)
    return t


def kernel(x, c, positions, w_ada, b_ada, w_in, conv_w, conv_b, rglru_wa, rglru_ba, rglru_wx, rglru_bx, rglru_lambda, q_norm_gain, k_norm_gain, lambda_q1, lambda_k1, lambda_q2, lambda_k2, subln_gain, w_proj_rnn, w_proj_attn, w_out, w_ff1, w_ff2):
    B, S, D = x.shape
    depth = w_in.shape[0]
    d_rnn = conv_w.shape[-1]
    d_qk = N_HEADS * 2 * HEAD_DIM
    d_attn = N_HEADS * V_DIM
    assert w_in.shape[-1] == 2 * d_rnn + 2 * d_qk + d_attn + N_BRANCH * D
    assert d_rnn == D and D % V7X_LANES == 0

    tm = _token_tile(S, 512)
    ts = _token_tile(S, 256)
    scale = HEAD_DIM ** -0.5

    cos_t, sin_t = _rope_tables(positions)
    o1, o2, o3, o4, o5 = d_rnn, 2 * d_rnn, 2 * d_rnn + d_qk, 2 * d_rnn + 2 * d_qk, 2 * d_rnn + 2 * d_qk + d_attn

    for l in range(depth):
        lam_init = 0.8 - 0.6 * math.exp(-0.3 * l)
        mod3 = _modulation(c, w_ada[l], b_ada[l]).reshape(B, N_MOD, D)

        wl = w_in[l]
        w_a = jnp.concatenate([wl[:, :o2], wl[:, o5:]], axis=1).astype(BF16)
        w_t = wl[:, o2:o5].T.astype(BF16)
        gq = jnp.broadcast_to((q_norm_gain[l] * (scale * _LOG2_E))[:, None], (HEAD_DIM, tm))
        gk = jnp.broadcast_to(k_norm_gain[l][:, None], (HEAD_DIM, tm))
        k_bound = math.sqrt(HEAD_DIM) * jnp.max(jnp.abs(k_norm_gain[l])) * _BOUND_SLACK
        q_bound = math.sqrt(HEAD_DIM) * jnp.max(jnp.abs(q_norm_gain[l])) * (scale * _LOG2_E) * _BOUND_SLACK
        xr, gr, gm, qt, k, vt = _in_projection(x, mod3, w_a, w_t, gq, gk, cos_t, sin_t, tm=tm, d_rnn=d_rnn)

        w_gate = jnp.concatenate([rglru_wa[l], rglru_wx[l]], axis=-1).astype(BF16)
        row = lambda v: v.reshape(1, -1)
        y_rnn = _rg_lru(xr, gr, conv_w[l], row(conv_b[l]), w_gate, row(rglru_ba[l]), row(rglru_bx[l]),
                        row(rglru_lambda[l]), ts=ts)

        lamv = jnp.stack([lambda_q1[l], lambda_k1[l], lambda_q2[l], lambda_k2[l]], axis=0)
        attn = functools.partial(_diff_attention, tq=tm, lam_init=lam_init)
        y_attn = lax.cond(
            q_bound * k_bound <= _MAX_ABS_SCORE_LOG2,
            functools.partial(attn, online=False), functools.partial(attn, online=True),
            lamv, qt, k, vt, row(subln_gain[l]))

        x = _post(x, y_rnn, y_attn, gm, mod3, w_proj_rnn[l].astype(BF16), w_proj_attn[l].astype(BF16),
                  w_out[l].astype(BF16), w_ff1[l].astype(BF16), w_ff2[l].astype(BF16), tm=tm)
    return x
```

```python
import functools
import math

import jax
import jax.numpy as jnp
from jax import lax
from jax.experimental import pallas as pl
from jax.experimental.pallas import tpu as pltpu

N_HEADS = 8
HEAD_DIM = 64
V_DIM = 2 * HEAD_DIM
ROPE_THETA = 10000.0
N_RNN_BLOCKS = 8
CONV_WIDTH = 4
LRU_C = 8.0
N_BRANCH = 2
N_MOD = 6
NORM_EPS = 1e-6

V7X_SUBLANES = 8
V7X_LANES = 128
V7X_VMEM_LIMIT_BYTES = 56 * 1024 * 1024

F32 = jnp.float32
BF16 = jnp.bfloat16

_LOG2_E = math.log2(math.e)
_BOUND_SLACK = 1.0 + 2.0 ** -6
_MAX_ABS_SCORE_LOG2 = 48.0


def _dot(a, b):
    return jnp.dot(a, b, preferred_element_type=F32)


def _dot_nt(a, b):
    return lax.dot_general(a, b, (((1,), (1,)), ((), ())), preferred_element_type=F32)


def _dot_tn(a, b):
    return lax.dot_general(a, b, (((0,), (0,)), ((), ())), preferred_element_type=F32)


def _resident(shape):
    zeros = (0,) * len(shape)
    return pl.BlockSpec(shape, lambda *_: zeros, pipeline_mode=pl.Buffered(1))


def _params(semantics):
    return pltpu.CompilerParams(dimension_semantics=semantics, vmem_limit_bytes=V7X_VMEM_LIMIT_BYTES)


def _mod_kernel(c_ref, w_ref, b_ref, lamv_ref, o_ref, lam_ref, *, lam_init):
    lv = lamv_ref[...]
    lam = (jnp.exp(jnp.sum(lv[0:1, :] * lv[1:2, :], axis=-1, keepdims=True))
           - jnp.exp(jnp.sum(lv[2:3, :] * lv[3:4, :], axis=-1, keepdims=True)) + lam_init)
    lam_ref[...] = jnp.broadcast_to(lam, lam_ref.shape)

    c = c_ref[...]
    ca = c * jax.nn.sigmoid(c)
    w = w_ref[...]
    ca_hi = ca.astype(BF16)
    ca_lo = (ca - ca_hi.astype(F32)).astype(BF16)
    w_hi = w.astype(BF16)
    w_lo = (w - w_hi.astype(F32)).astype(BF16)
    acc = _dot(ca_hi, w_hi) + _dot(ca_lo, w_hi) + _dot(ca_hi, w_lo)
    o_ref[...] = acc + b_ref[...]


def _modulation(c, w_ada, b_ada, lamv, *, lam_init, lam_width):
    B, D = c.shape
    n = w_ada.shape[1]
    tn = n // 4
    return pl.pallas_call(
        functools.partial(_mod_kernel, lam_init=lam_init),
        grid=(n // tn,),
        in_specs=[
            pl.BlockSpec((B, D), lambda j: (0, 0)),
            pl.BlockSpec((D, tn), lambda j: (0, j)),
            pl.BlockSpec((1, tn), lambda j: (0, j)),
            pl.BlockSpec(lamv.shape, lambda j: (0, 0)),
        ],
        out_specs=[pl.BlockSpec((B, tn), lambda j: (0, j)), pl.BlockSpec((1, lam_width), lambda j: (0, 0))],
        out_shape=[jax.ShapeDtypeStruct((B, n), F32), jax.ShapeDtypeStruct((1, lam_width), F32)],
        compiler_params=_params(("arbitrary",)),
        name="modulation",
    )(c, w_ada, b_ada.reshape(1, n), lamv)


def _rope_kernel(pos_ref, invf_ref, cos_ref, sin_ref):
    ang = invf_ref[...] * pos_ref[...].astype(F32)
    cos_ref[...] = jnp.cos(ang)
    sin_ref[...] = jnp.sin(ang)


def _rope_tables(positions):
    B, S = positions.shape
    half = HEAD_DIM // 2
    inv_freq = (ROPE_THETA ** (-jnp.arange(0, HEAD_DIM, 2, dtype=F32) / HEAD_DIM)).reshape(half, 1)
    ts = S // 4 if S % (4 * V7X_LANES) == 0 else S
    out = jax.ShapeDtypeStruct((B, half, S), F32)
    return pl.pallas_call(
        _rope_kernel,
        grid=(B, S // ts),
        in_specs=[
            pl.BlockSpec((None, 1, ts), lambda b, i: (b, 0, i)),
            pl.BlockSpec((half, 1), lambda b, i: (0, 0)),
        ],
        out_specs=[pl.BlockSpec((None, half, ts), lambda b, i: (b, 0, i))] * 2,
        out_shape=[out, out],
        compiler_params=_params(("arbitrary", "arbitrary")),
        name="rope_tables",
    )(positions.reshape(B, 1, S), inv_freq)


def _ada_norm(x, shift, scale):
    ms = jnp.mean(x * x, axis=-1, keepdims=True)
    return (x * lax.rsqrt(ms + NORM_EPS)) * (1.0 + scale) + shift


def _norm_rope_fm(t, gain, cos, sin):
    half = HEAD_DIM // 2
    outs = []
    for c in range(2):
        blk = t[c * HEAD_DIM:(c + 1) * HEAD_DIM, :]
        ms = jnp.mean(blk * blk, axis=0, keepdims=True)
        n = blk * lax.rsqrt(ms + NORM_EPS) * gain
        x1 = n[:half, :]
        x2 = n[half:, :]
        outs.append(x1 * cos - x2 * sin)
        outs.append(x2 * cos + x1 * sin)
    return jnp.concatenate(outs, axis=0)


def _inproj_kernel(x_ref, mod_ref, wa_ref, wt_ref, gq_ref, gk_ref, cos_ref, sin_ref,
                   xr_ref, gr_ref, gm_ref, qt_ref, k_ref, vt_ref, *, d_rnn, d_qk, d_attn):
    h = _ada_norm(x_ref[...], mod_ref[0:1, :], mod_ref[1:2, :]).astype(BF16)

    def token_major(ref, col):
        for c0 in range(0, ref.shape[-1], d_rnn):
            ref[:, c0:c0 + d_rnn] = _dot(h, wa_ref[:, col + c0:col + c0 + d_rnn]).astype(ref.dtype)

    token_major(xr_ref, 0)
    token_major(gr_ref, d_rnn)
    q_raw = _dot_nt(wt_ref[0:d_qk, :], h)
    k_raw = _dot_nt(wt_ref[d_qk:2 * d_qk, :], h)
    token_major(gm_ref, 2 * d_rnn)
    vt_ref[...] = _dot_nt(wt_ref[2 * d_qk:2 * d_qk + d_attn, :], h).astype(vt_ref.dtype)

    cos = cos_ref[...]
    sin = sin_ref[...]
    hd2 = 2 * HEAD_DIM
    for hh in range(N_HEADS):
        rows = slice(hh * hd2, (hh + 1) * hd2)
        qt_ref[rows, :] = _norm_rope_fm(q_raw[rows, :], gq_ref[...], cos, sin).astype(qt_ref.dtype)
        kh = _norm_rope_fm(k_raw[rows, :], gk_ref[...], cos, sin)
        k_ref[:, rows] = kh.T.astype(k_ref.dtype)


def _in_projection(x, mod3, w_a, w_t, gq, gk, cos_t, sin_t, *, tm, d_rnn):
    B, S, D = x.shape
    d_qk = N_HEADS * 2 * HEAD_DIM
    d_attn = N_HEADS * V_DIM
    d_gm = N_BRANCH * D
    half = HEAD_DIM // 2
    nt = S // tm
    tok = lambda w: pl.BlockSpec((None, tm, w), lambda b, i: (b, i, 0))
    fm = lambda r: pl.BlockSpec((None, r, tm), lambda b, i: (b, 0, i))
    kernel = functools.partial(_inproj_kernel, d_rnn=d_rnn, d_qk=d_qk, d_attn=d_attn)
    return pl.pallas_call(
        kernel,
        grid=(B, nt),
        in_specs=[
            tok(D),
            pl.BlockSpec((None, N_MOD, D), lambda b, i: (b, 0, 0)),
            _resident(w_a.shape),
            _resident(w_t.shape),
            _resident(gq.shape),
            _resident(gk.shape),
            fm(half),
            fm(half),
        ],
        out_specs=[
            tok(d_rnn), tok(d_rnn), tok(d_gm),
            fm(d_qk),
            tok(d_qk),
            pl.BlockSpec((None, None, d_attn, tm), lambda b, i: (b, i, 0, 0)),
        ],
        out_shape=[
            jax.ShapeDtypeStruct((B, S, d_rnn), BF16),
            jax.ShapeDtypeStruct((B, S, d_rnn), BF16),
            jax.ShapeDtypeStruct((B, S, d_gm), BF16),
            jax.ShapeDtypeStruct((B, d_qk, S), BF16),
            jax.ShapeDtypeStruct((B, S, d_qk), BF16),
            jax.ShapeDtypeStruct((B, nt, d_attn, tm), BF16),
        ],
        compiler_params=_params(("arbitrary", "arbitrary")),
        name="in_projection",
    )(x, mod3, w_a, w_t, gq, gk, cos_t, sin_t)


_SCAN_TILE = 256


def _scan_permutations(ts):
    r = jnp.arange(ts)
    tau = (r % V7X_SUBLANES) * (ts // V7X_SUBLANES) + r // V7X_SUBLANES
    cols = jnp.arange(ts)[None, :]
    perm = jnp.stack([(cols == (tau[:, None] - d)).astype(BF16) for d in range(CONV_WIDTH)])
    return perm, perm[0].T


def _gelu_tanh(x):
    k2 = 2.0 * math.sqrt(2.0 / math.pi) * _LOG2_E
    e = jnp.exp2(x * ((-k2) - (k2 * 0.044715) * (x * x)))
    return x / (1.0 + e)


def _rglru_tile(xb, gb, tail, carry, perm_ref, unperm_ref, cw_ref, cb_ref, wg_ref, ba_ref, bx_ref, lam_ref):
    ts, C = xb.shape
    blk = C // N_RNN_BLOCKS
    ng = ts // V7X_SUBLANES
    sub = lax.broadcasted_iota(jnp.int32, (V7X_SUBLANES, C), 0)

    conv = cb_ref[...] + _dot(perm_ref[0], xb) * cw_ref[CONV_WIDTH - 1:CONV_WIDTH, :]
    for d in range(1, CONV_WIDTH):
        conv = conv + _dot(perm_ref[d], xb) * cw_ref[CONV_WIDTH - 1 - d:CONV_WIDTH - d, :]
    top = []
    for j in range(CONV_WIDTH - 1):
        fix = None
        for d in range(j + 1, CONV_WIDTH):
            t = tail[V7X_SUBLANES - d + j:V7X_SUBLANES - d + j + 1, :] * cw_ref[CONV_WIDTH - 1 - d:CONV_WIDTH - d, :]
            fix = t if fix is None else fix + t
        rows = conv[j * V7X_SUBLANES:(j + 1) * V7X_SUBLANES, :]
        top.append(jnp.where(sub == 0, rows + fix, rows))
    conv = jnp.concatenate(top + [conv[(CONV_WIDTH - 1) * V7X_SUBLANES:, :]], axis=0)

    convb = conv.astype(BF16)
    pre = [_dot(convb[:, n * blk:(n + 1) * blk], wg_ref[n]) for n in range(N_RNN_BLOCKS)]
    r = jax.nn.sigmoid(jnp.concatenate([p[:, :blk] for p in pre], axis=1) + ba_ref[...])
    i = jax.nn.sigmoid(jnp.concatenate([p[:, blk:] for p in pre], axis=1) + bx_ref[...])

    z = -lam_ref[...]
    softplus = jnp.maximum(z, 0.0) + jnp.log1p(jnp.exp(-jnp.abs(z)))
    log_a = (-LRU_C) * r * softplus
    a = jnp.exp(log_a)
    u = jnp.exp2(0.5 * jnp.log2(-jnp.tanh(log_a) * (a * a + 1.0))) * (i * conv)

    hl = pc = None
    hls, pcs = [], []
    for g in range(ng):
        sl = slice(g * V7X_SUBLANES, (g + 1) * V7X_SUBLANES)
        hl = u[sl, :] if hl is None else a[sl, :] * hl + u[sl, :]
        pc = a[sl, :] if pc is None else a[sl, :] * pc
        hls.append(hl)
        pcs.append(pc)
    f, gt = hl, pc
    s = 1
    while s < V7X_SUBLANES:
        ok = sub >= s
        f = jnp.where(ok, f + gt * pltpu.roll(f, s, 0), f)
        gt = jnp.where(ok, gt * pltpu.roll(gt, s, 0), gt)
        s *= 2
    after = f + gt * carry
    init = jnp.where(sub == 0, carry, pltpu.roll(after, 1, 0))
    new_carry = jnp.broadcast_to(after[V7X_SUBLANES - 1:, :], after.shape)
    h = jnp.concatenate([hls[g] + pcs[g] * init for g in range(ng)], axis=0)

    y = (h * _gelu_tanh(_dot(perm_ref[0], gb))).astype(BF16)
    return _dot(unperm_ref[...], y).astype(BF16), new_carry


def _rglru_kernel(xr_ref, gr_ref, perm_ref, unperm_ref, cw_ref, cb_ref, wg_ref, ba_ref, bx_ref, lam_ref,
                  y_ref, tail_ref, h_ref):
    rows = xr_ref.shape[0]
    ts = perm_ref.shape[-1]

    @pl.when(pl.program_id(1) == 0)
    def _():
        tail_ref[...] = jnp.zeros_like(tail_ref)
        h_ref[...] = jnp.zeros_like(h_ref)

    tail = tail_ref[...]
    carry = h_ref[...]
    for t0 in range(0, rows, ts):
        xb = xr_ref[t0:t0 + ts, :]
        y, carry = _rglru_tile(xb, gr_ref[t0:t0 + ts, :], tail, carry, perm_ref, unperm_ref,
                               cw_ref, cb_ref, wg_ref, ba_ref, bx_ref, lam_ref)
        y_ref[t0:t0 + ts, :] = y
        tail = xb[ts - V7X_SUBLANES:, :].astype(F32)
    tail_ref[...] = tail
    h_ref[...] = carry


def _rg_lru(xr, gr, conv_w, conv_b, w_gate, b_a, b_x, lam, *, rows):
    B, S, C = xr.shape
    perm, unperm = _scan_permutations(_SCAN_TILE)
    tok = pl.BlockSpec((None, rows, C), lambda b, i: (b, i, 0))
    row = lambda a: _resident(a.shape)
    return pl.pallas_call(
        _rglru_kernel,
        grid=(B, S // rows),
        in_specs=[tok, tok, row(perm), row(unperm), row(conv_w), row(conv_b), row(w_gate), row(b_a), row(b_x),
                  row(lam)],
        out_specs=tok,
        out_shape=jax.ShapeDtypeStruct((B, S, C), BF16),
        scratch_shapes=[pltpu.VMEM((V7X_SUBLANES, C), F32), pltpu.VMEM((V7X_SUBLANES, C), F32)],
        compiler_params=_params(("arbitrary", "arbitrary")),
        name="rg_lru",
    )(xr, gr, perm, unperm, conv_w, conv_b, w_gate, b_a, b_x, lam)


_KV_UNROLL = 4


def _attn_kernel(lam_ref, qt_ref, k_ref, vt_ref, g_ref, o_ref, qbd_ref, m_ref, l_ref, acc_ref,
                 *, tq, tk, online):
    i = pl.program_id(2)

    zero = jnp.zeros((HEAD_DIM, tq), qbd_ref.dtype)
    qbd_ref[:HEAD_DIM, :tq] = qt_ref[:HEAD_DIM, :]
    qbd_ref[:HEAD_DIM, tq:] = zero
    qbd_ref[HEAD_DIM:, :tq] = zero
    qbd_ref[HEAD_DIM:, tq:] = qt_ref[HEAD_DIM:, :]
    if online:
        m_ref[...] = jnp.full(m_ref.shape, -jnp.inf, F32)
    l_ref[...] = jnp.zeros_like(l_ref)
    acc_ref[...] = jnp.zeros_like(acc_ref)

    def scores(j, masked):
        start = pl.multiple_of(j * tk, tk)
        s = _dot(k_ref[pl.ds(start, tk), :], qbd_ref[...])
        if masked:
            key = lax.broadcasted_iota(jnp.int32, s.shape, 0)
            col = lax.broadcasted_iota(jnp.int32, s.shape, 1)
            qry = jnp.where(col >= tq, col - tq, col)
            s = jnp.where(key <= qry, s, -jnp.inf)
        return s

    def colsum8(p):
        return jnp.sum(p.reshape(tk // V7X_SUBLANES, V7X_SUBLANES, 2 * tq), axis=0)

    def online_block(j, masked):
        s = scores(j, masked)
        m_old = m_ref[...]
        m_new = jnp.maximum(m_old, jnp.max(s, axis=0, keepdims=True))
        alpha = jnp.exp2(m_old - m_new)
        p = jnp.exp2(s - m_new)
        l_ref[...] = alpha * l_ref[...] + colsum8(p)
        acc_ref[...] = alpha * acc_ref[...] + _dot(vt_ref[j], p.astype(vt_ref.dtype))
        m_ref[...] = m_new

    def streaming_blocks(base, count, mask_last):
        lsum = pv = None
        for u in range(count):
            p = jnp.exp2(scores(base + u, mask_last and u == count - 1))
            ps = colsum8(p)
            d = _dot(vt_ref[base + u], p.astype(vt_ref.dtype))
            lsum = ps if lsum is None else lsum + ps
            pv = d if pv is None else pv + d
        l_ref[...] += lsum
        acc_ref[...] += pv

    if online:
        def body(j, carry):
            online_block(j, False)
            return carry

        lax.fori_loop(0, i, body, 0)
        online_block(i, True)
    else:
        def body(t, carry):
            streaming_blocks(t * _KV_UNROLL, _KV_UNROLL, False)
            return carry

        rem = lax.rem(i, _KV_UNROLL)
        lax.fori_loop(0, lax.div(i, _KV_UNROLL), body, 0)
        for r in range(_KV_UNROLL):
            @pl.when(rem == r)
            def _(r=r):
                streaming_blocks(i - r, r + 1, True)

    inv_l = 1.0 / jnp.sum(l_ref[...], axis=0, keepdims=True)
    ot = acc_ref[:, :tq] * inv_l[:, :tq] - acc_ref[:, tq:] * (lam_ref[...] * inv_l[:, tq:])
    ms = jnp.mean(ot * ot, axis=0, keepdims=True)
    o_ref[...] = (ot * lax.rsqrt(ms + NORM_EPS) * g_ref[...]).astype(o_ref.dtype)


def _diff_attention(lam, qt, k, vt, subln_gain, *, tq, online):
    B, d_qk, S = qt.shape
    nkv = vt.shape[1]
    tk = vt.shape[3]
    assert tq == tk and lam.shape == (1, tq)
    hd2 = 2 * HEAD_DIM
    kernel = functools.partial(_attn_kernel, tq=tq, tk=tk, online=online)
    return pl.pallas_call(
        kernel,
        grid=(B, N_HEADS, S // tq),
        in_specs=[
            pl.BlockSpec(lam.shape, lambda b, h, i: (0, 0)),
            pl.BlockSpec((None, hd2, tq), lambda b, h, i: (b, h, i)),
            pl.BlockSpec((None, S, hd2), lambda b, h, i: (b, 0, h)),
            pl.BlockSpec((None, nkv, V_DIM, tk), lambda b, h, i: (b, 0, h, 0)),
            pl.BlockSpec((V_DIM, tq), lambda b, h, i: (0, 0)),
        ],
        out_specs=pl.BlockSpec((None, V_DIM, tq), lambda b, h, i: (b, h, i)),
        out_shape=jax.ShapeDtypeStruct((B, N_HEADS * V_DIM, S), BF16),
        scratch_shapes=[
            pltpu.VMEM((hd2, 2 * tq), BF16),
            pltpu.VMEM((1, 2 * tq), F32),
            pltpu.VMEM((V7X_SUBLANES, 2 * tq), F32),
            pltpu.VMEM((V_DIM, 2 * tq), F32),
        ],
        compiler_params=_params(("arbitrary", "arbitrary", "arbitrary")),
        name="diff_attention_online" if online else "diff_attention",
    )(lam, qt, k, vt, subln_gain)


def _post_kernel(x_ref, yr_ref, yat_ref, gm_ref, mod_ref, wpr_ref, wpa_ref, wo_ref, w1_ref, w2_ref, o_ref):
    D = x_ref.shape[-1]
    gates = jax.nn.sigmoid(gm_ref[...].astype(F32))
    merged = gates[:, :D] * _dot(yr_ref[...], wpr_ref[...]) + gates[:, D:] * _dot_tn(yat_ref[...], wpa_ref[...])
    x1 = x_ref[...] + mod_ref[2:3, :] * _dot(merged.astype(BF16), wo_ref[...])

    h2 = _ada_norm(x1, mod_ref[3:4, :], mod_ref[4:5, :]).astype(BF16)
    d_ff = w1_ref.shape[1]
    ff = None
    for c0 in range(0, d_ff, D):
        hid = jnp.square(jnp.maximum(_dot(h2, w1_ref[:, c0:c0 + D]), 0.0)).astype(BF16)
        part = _dot(hid, w2_ref[c0:c0 + D, :])
        ff = part if ff is None else ff + part
    o_ref[...] = x1 + mod_ref[5:6, :] * ff


def _post(x, y_rnn, y_attn_t, gm, mod3, wpr, wpa, wo, w1, w2, *, tm):
    B, S, D = x.shape
    tok = lambda w: pl.BlockSpec((None, tm, w), lambda b, i: (b, i, 0))
    return pl.pallas_call(
        _post_kernel,
        grid=(B, S // tm),
        in_specs=[
            tok(D), tok(y_rnn.shape[-1]),
            pl.BlockSpec((None, y_attn_t.shape[1], tm), lambda b, i: (b, 0, i)),
            tok(gm.shape[-1]),
            pl.BlockSpec((None, N_MOD, D), lambda b, i: (b, 0, 0)),
            _resident(wpr.shape), _resident(wpa.shape), _resident(wo.shape),
            _resident(w1.shape), _resident(w2.shape),
        ],
        out_specs=tok(D),
        out_shape=jax.ShapeDtypeStruct((B, S, D), F32),
        compiler_params=_params(("arbitrary", "arbitrary")),
        name="post",
    )(x, y_rnn, y_attn_t, gm, mod3, wpr, wpa, wo, w1, w2)


def _token_tile(S, want):
    t = min(S, want)
    assert S % t == 0, (S, t)
    return t


def kernel(x, c, positions, w_ada, b_ada, w_in, conv_w, conv_b, rglru_wa, rglru_ba, rglru_wx, rglru_bx, rglru_lambda, q_norm_gain, k_norm_gain, lambda_q1, lambda_k1, lambda_q2, lambda_k2, subln_gain, w_proj_rnn, w_proj_attn, w_out, w_ff1, w_ff2):
    B, S, D = x.shape
    depth = w_in.shape[0]
    d_rnn = conv_w.shape[-1]
    d_qk = N_HEADS * 2 * HEAD_DIM
    d_attn = N_HEADS * V_DIM
    assert w_in.shape[-1] == 2 * d_rnn + 2 * d_qk + d_attn + N_BRANCH * D
    assert d_rnn == D and D % V7X_LANES == 0

    tm = _token_tile(S, 512)
    assert tm % _SCAN_TILE == 0
    scale = HEAD_DIM ** -0.5

    cos_t, sin_t = _rope_tables(positions)
    o1, o2, o3, o4, o5 = d_rnn, 2 * d_rnn, 2 * d_rnn + d_qk, 2 * d_rnn + 2 * d_qk, 2 * d_rnn + 2 * d_qk + d_attn

    for l in range(depth):
        lam_init = 0.8 - 0.6 * math.exp(-0.3 * l)
        lamv = jnp.stack([lambda_q1[l], lambda_k1[l], lambda_q2[l], lambda_k2[l]], axis=0)
        mod, lam = _modulation(c, w_ada[l], b_ada[l], lamv, lam_init=lam_init, lam_width=tm)
        mod3 = mod.reshape(B, N_MOD, D)

        wl = w_in[l]
        w_a = jnp.concatenate([wl[:, :o2], wl[:, o5:]], axis=1).astype(BF16)
        w_t = wl[:, o2:o5].T.astype(BF16)
        gq = jnp.broadcast_to((q_norm_gain[l] * (scale * _LOG2_E))[:, None], (HEAD_DIM, tm))
        gk = jnp.broadcast_to(k_norm_gain[l][:, None], (HEAD_DIM, tm))
        k_bound = math.sqrt(HEAD_DIM) * jnp.max(jnp.abs(k_norm_gain[l])) * _BOUND_SLACK
        q_bound = math.sqrt(HEAD_DIM) * jnp.max(jnp.abs(q_norm_gain[l])) * (scale * _LOG2_E) * _BOUND_SLACK
        xr, gr, gm, qt, k, vt = _in_projection(x, mod3, w_a, w_t, gq, gk, cos_t, sin_t, tm=tm, d_rnn=d_rnn)

        w_gate = jnp.concatenate([rglru_wa[l], rglru_wx[l]], axis=-1).astype(BF16)
        row = lambda v: v.reshape(1, -1)
        y_rnn = _rg_lru(xr, gr, conv_w[l], row(conv_b[l]), w_gate, row(rglru_ba[l]), row(rglru_bx[l]),
                        row(rglru_lambda[l]), rows=tm)

        attn = functools.partial(_diff_attention, tq=tm)
        sub_gain = jnp.broadcast_to((subln_gain[l] * (1.0 - lam_init))[:, None], (V_DIM, tm))
        y_attn_t = lax.cond(
            q_bound * k_bound <= _MAX_ABS_SCORE_LOG2,
            functools.partial(attn, online=False), functools.partial(attn, online=True),
            lam, qt, k, vt, sub_gain)

        x = _post(x, y_rnn, y_attn_t, gm, mod3, w_proj_rnn[l].astype(BF16), w_proj_attn[l].astype(BF16),
                  w_out[l].astype(BF16), w_ff1[l].astype(BF16), w_ff2[l].astype(BF16), tm=tm)
    return x
```

```python
import functools
import math

import jax
import jax.numpy as jnp
from jax import lax
from jax.experimental import pallas as pl
from jax.experimental.pallas import tpu as pltpu

N_HEADS = 8
HEAD_DIM = 64
V_DIM = 2 * HEAD_DIM
ROPE_THETA = 10000.0
N_RNN_BLOCKS = 8
CONV_WIDTH = 4
LRU_C = 8.0
N_BRANCH = 2
N_MOD = 6
NORM_EPS = 1e-6

V7X_SUBLANES = 8
V7X_LANES = 128
V7X_VMEM_LIMIT_BYTES = 56 * 1024 * 1024

F32 = jnp.float32
BF16 = jnp.bfloat16

_LOG2_E = math.log2(math.e)
_BOUND_SLACK = 1.0 + 2.0 ** -6
_MAX_ABS_SCORE_LOG2 = 48.0


def _dot(a, b):
    return jnp.dot(a, b, preferred_element_type=F32)


def _dot_nt(a, b):
    return lax.dot_general(a, b, (((1,), (1,)), ((), ())), preferred_element_type=F32)


def _dot_tn(a, b):
    return lax.dot_general(a, b, (((0,), (0,)), ((), ())), preferred_element_type=F32)


def _resident(shape):
    zeros = (0,) * len(shape)
    return pl.BlockSpec(shape, lambda *_: zeros, pipeline_mode=pl.Buffered(1))


def _params(semantics):
    return pltpu.CompilerParams(dimension_semantics=semantics, vmem_limit_bytes=V7X_VMEM_LIMIT_BYTES)


def _mod_kernel(c_ref, w_ref, b_ref, lamv_ref, o_ref, lam_ref, *, lam_init):
    lv = lamv_ref[...]
    lam = (jnp.exp(jnp.sum(lv[0:1, :] * lv[1:2, :], axis=-1, keepdims=True))
           - jnp.exp(jnp.sum(lv[2:3, :] * lv[3:4, :], axis=-1, keepdims=True)) + lam_init)
    lam_ref[...] = jnp.broadcast_to(lam, lam_ref.shape)

    c = c_ref[...]
    ca = c * jax.nn.sigmoid(c)
    w = w_ref[...]
    ca_hi = ca.astype(BF16)
    ca_lo = (ca - ca_hi.astype(F32)).astype(BF16)
    w_hi = w.astype(BF16)
    w_lo = (w - w_hi.astype(F32)).astype(BF16)
    acc = _dot(ca_hi, w_hi) + _dot(ca_lo, w_hi) + _dot(ca_hi, w_lo)
    o_ref[...] = acc + b_ref[...]


def _modulation(c, w_ada, b_ada, lamv, *, lam_init, lam_width):
    B, D = c.shape
    n = w_ada.shape[1]
    tn = n // 4
    return pl.pallas_call(
        functools.partial(_mod_kernel, lam_init=lam_init),
        grid=(n // tn,),
        in_specs=[
            pl.BlockSpec((B, D), lambda j: (0, 0)),
            pl.BlockSpec((D, tn), lambda j: (0, j)),
            pl.BlockSpec((1, tn), lambda j: (0, j)),
            pl.BlockSpec(lamv.shape, lambda j: (0, 0)),
        ],
        out_specs=[pl.BlockSpec((B, tn), lambda j: (0, j)), pl.BlockSpec((1, lam_width), lambda j: (0, 0))],
        out_shape=[jax.ShapeDtypeStruct((B, n), F32), jax.ShapeDtypeStruct((1, lam_width), F32)],
        compiler_params=_params(("arbitrary",)),
        name="modulation",
    )(c, w_ada, b_ada.reshape(1, n), lamv)


def _rope_kernel(pos_ref, invf_ref, cos_ref, sin_ref):
    ang = invf_ref[...] * pos_ref[...].astype(F32)
    cos_ref[...] = jnp.cos(ang)
    sin_ref[...] = jnp.sin(ang)


def _rope_tables(positions):
    B, S = positions.shape
    half = HEAD_DIM // 2
    inv_freq = (ROPE_THETA ** (-jnp.arange(0, HEAD_DIM, 2, dtype=F32) / HEAD_DIM)).reshape(half, 1)
    ts = S // 4 if S % (4 * V7X_LANES) == 0 else S
    out = jax.ShapeDtypeStruct((B, half, S), F32)
    return pl.pallas_call(
        _rope_kernel,
        grid=(B, S // ts),
        in_specs=[
            pl.BlockSpec((None, 1, ts), lambda b, i: (b, 0, i)),
            pl.BlockSpec((half, 1), lambda b, i: (0, 0)),
        ],
        out_specs=[pl.BlockSpec((None, half, ts), lambda b, i: (b, 0, i))] * 2,
        out_shape=[out, out],
        compiler_params=_params(("arbitrary", "arbitrary")),
        name="rope_tables",
    )(positions.reshape(B, 1, S), inv_freq)


def _ada_norm(x, shift, scale):
    ms = jnp.mean(x * x, axis=-1, keepdims=True)
    return (x * lax.rsqrt(ms + NORM_EPS)) * (1.0 + scale) + shift


def _norm_rope_fm(t, gain, cos, sin):
    half = HEAD_DIM // 2
    outs = []
    for c in range(2):
        blk = t[c * HEAD_DIM:(c + 1) * HEAD_DIM, :]
        ms = jnp.mean(blk * blk, axis=0, keepdims=True)
        n = blk * lax.rsqrt(ms + NORM_EPS) * gain
        x1 = n[:half, :]
        x2 = n[half:, :]
        outs.append(x1 * cos - x2 * sin)
        outs.append(x2 * cos + x1 * sin)
    return jnp.concatenate(outs, axis=0)


def _inproj_kernel(x_ref, mod_ref, wa_ref, wt_ref, gq_ref, gk_ref, cos_ref, sin_ref,
                   xr_ref, gr_ref, gm_ref, qt_ref, k_ref, vt_ref, *, d_rnn, d_qk, d_attn):
    h = _ada_norm(x_ref[...], mod_ref[0:1, :], mod_ref[1:2, :]).astype(BF16)

    def token_major(ref, col):
        for c0 in range(0, ref.shape[-1], d_rnn):
            ref[:, c0:c0 + d_rnn] = _dot(h, wa_ref[:, col + c0:col + c0 + d_rnn]).astype(ref.dtype)

    token_major(xr_ref, 0)
    token_major(gr_ref, d_rnn)
    q_raw = _dot_nt(wt_ref[0:d_qk, :], h)
    k_raw = _dot_nt(wt_ref[d_qk:2 * d_qk, :], h)
    token_major(gm_ref, 2 * d_rnn)
    vt_ref[...] = _dot_nt(wt_ref[2 * d_qk:2 * d_qk + d_attn, :], h).astype(vt_ref.dtype)

    cos = cos_ref[...]
    sin = sin_ref[...]
    hd2 = 2 * HEAD_DIM
    for hh in range(N_HEADS):
        rows = slice(hh * hd2, (hh + 1) * hd2)
        qt_ref[rows, :] = _norm_rope_fm(q_raw[rows, :], gq_ref[...], cos, sin).astype(qt_ref.dtype)
        kh = _norm_rope_fm(k_raw[rows, :], gk_ref[...], cos, sin)
        k_ref[:, rows] = kh.T.astype(k_ref.dtype)


def _in_projection(x, mod3, w_a, w_t, gq, gk, cos_t, sin_t, *, tm, d_rnn):
    B, S, D = x.shape
    d_qk = N_HEADS * 2 * HEAD_DIM
    d_attn = N_HEADS * V_DIM
    d_gm = N_BRANCH * D
    half = HEAD_DIM // 2
    nt = S // tm
    tok = lambda w: pl.BlockSpec((None, tm, w), lambda b, i: (b, i, 0))
    fm = lambda r: pl.BlockSpec((None, r, tm), lambda b, i: (b, 0, i))
    kernel = functools.partial(_inproj_kernel, d_rnn=d_rnn, d_qk=d_qk, d_attn=d_attn)
    return pl.pallas_call(
        kernel,
        grid=(B, nt),
        in_specs=[
            tok(D),
            pl.BlockSpec((None, N_MOD, D), lambda b, i: (b, 0, 0)),
            _resident(w_a.shape),
            _resident(w_t.shape),
            _resident(gq.shape),
            _resident(gk.shape),
            fm(half),
            fm(half),
        ],
        out_specs=[
            tok(d_rnn), tok(d_rnn), tok(d_gm),
            pl.BlockSpec((None, None, d_qk, tm), lambda b, i: (b, i, 0, 0)),
            tok(d_qk),
            pl.BlockSpec((None, None, d_attn, tm), lambda b, i: (b, i, 0, 0)),
        ],
        out_shape=[
            jax.ShapeDtypeStruct((B, S, d_rnn), BF16),
            jax.ShapeDtypeStruct((B, S, d_rnn), BF16),
            jax.ShapeDtypeStruct((B, S, d_gm), BF16),
            jax.ShapeDtypeStruct((B, nt, d_qk, tm), BF16),
            jax.ShapeDtypeStruct((B, S, d_qk), BF16),
            jax.ShapeDtypeStruct((B, nt, d_attn, tm), BF16),
        ],
        compiler_params=_params(("arbitrary", "arbitrary")),
        name="in_projection",
    )(x, mod3, w_a, w_t, gq, gk, cos_t, sin_t)


_SCAN_TILE = 256


def _scan_permutations(ts):
    r = jnp.arange(ts)
    tau = (r % V7X_SUBLANES) * (ts // V7X_SUBLANES) + r // V7X_SUBLANES
    cols = jnp.arange(ts)[None, :]
    perm = jnp.stack([(cols == (tau[:, None] - d)).astype(BF16) for d in range(CONV_WIDTH)])
    return perm, perm[0].T


def _gelu_tanh(x):
    k2 = 2.0 * math.sqrt(2.0 / math.pi) * _LOG2_E
    e = jnp.exp2(x * ((-k2) - (k2 * 0.044715) * (x * x)))
    return x / (1.0 + e)


def _rglru_tile(xb, gb, tail, carry, perm_ref, unperm_ref, cw_ref, cb_ref, wg_ref, ba_ref, bx_ref, lam_ref):
    ts, C = xb.shape
    blk = C // N_RNN_BLOCKS
    ng = ts // V7X_SUBLANES
    sub = lax.broadcasted_iota(jnp.int32, (V7X_SUBLANES, C), 0)

    conv = cb_ref[...] + _dot(perm_ref[0], xb) * cw_ref[CONV_WIDTH - 1:CONV_WIDTH, :]
    for d in range(1, CONV_WIDTH):
        conv = conv + _dot(perm_ref[d], xb) * cw_ref[CONV_WIDTH - 1 - d:CONV_WIDTH - d, :]
    top = []
    for j in range(CONV_WIDTH - 1):
        fix = None
        for d in range(j + 1, CONV_WIDTH):
            t = tail[V7X_SUBLANES - d + j:V7X_SUBLANES - d + j + 1, :] * cw_ref[CONV_WIDTH - 1 - d:CONV_WIDTH - d, :]
            fix = t if fix is None else fix + t
        rows = conv[j * V7X_SUBLANES:(j + 1) * V7X_SUBLANES, :]
        top.append(jnp.where(sub == 0, rows + fix, rows))
    conv = jnp.concatenate(top + [conv[(CONV_WIDTH - 1) * V7X_SUBLANES:, :]], axis=0)

    convb = conv.astype(BF16)
    pre = [_dot(convb[:, n * blk:(n + 1) * blk], wg_ref[n]) for n in range(N_RNN_BLOCKS)]
    r = jax.nn.sigmoid(jnp.concatenate([p[:, :blk] for p in pre], axis=1) + ba_ref[...])
    i = jax.nn.sigmoid(jnp.concatenate([p[:, blk:] for p in pre], axis=1) + bx_ref[...])

    z = -lam_ref[...]
    softplus = jnp.maximum(z, 0.0) + jnp.log1p(jnp.exp(-jnp.abs(z)))
    log_a = (-LRU_C) * r * softplus
    a = jnp.exp(log_a)
    u = jnp.exp2(0.5 * jnp.log2(-jnp.tanh(log_a) * (a * a + 1.0))) * (i * conv)

    hl = pc = None
    hls, pcs = [], []
    for g in range(ng):
        sl = slice(g * V7X_SUBLANES, (g + 1) * V7X_SUBLANES)
        hl = u[sl, :] if hl is None else a[sl, :] * hl + u[sl, :]
        pc = a[sl, :] if pc is None else a[sl, :] * pc
        hls.append(hl)
        pcs.append(pc)
    f, gt = hl, pc
    s = 1
    while s < V7X_SUBLANES:
        ok = sub >= s
        f = jnp.where(ok, f + gt * pltpu.roll(f, s, 0), f)
        gt = jnp.where(ok, gt * pltpu.roll(gt, s, 0), gt)
        s *= 2
    after = f + gt * carry
    init = jnp.where(sub == 0, carry, pltpu.roll(after, 1, 0))
    new_carry = jnp.broadcast_to(after[V7X_SUBLANES - 1:, :], after.shape)
    h = jnp.concatenate([hls[g] + pcs[g] * init for g in range(ng)], axis=0)

    y = (h * _gelu_tanh(_dot(perm_ref[0], gb))).astype(BF16)
    return _dot(unperm_ref[...], y).astype(BF16), new_carry


def _rglru_kernel(xr_ref, gr_ref, perm_ref, unperm_ref, cw_ref, cb_ref, wg_ref, ba_ref, bx_ref, lam_ref,
                  y_ref, tail_ref, h_ref):
    rows = xr_ref.shape[0]
    ts = perm_ref.shape[-1]

    @pl.when(pl.program_id(1) == 0)
    def _():
        tail_ref[...] = jnp.zeros_like(tail_ref)
        h_ref[...] = jnp.zeros_like(h_ref)

    tail = tail_ref[...]
    carry = h_ref[...]
    for t0 in range(0, rows, ts):
        xb = xr_ref[t0:t0 + ts, :]
        y, carry = _rglru_tile(xb, gr_ref[t0:t0 + ts, :], tail, carry, perm_ref, unperm_ref,
                               cw_ref, cb_ref, wg_ref, ba_ref, bx_ref, lam_ref)
        y_ref[t0:t0 + ts, :] = y
        tail = xb[ts - V7X_SUBLANES:, :].astype(F32)
    tail_ref[...] = tail
    h_ref[...] = carry


def _rg_lru(xr, gr, conv_w, conv_b, w_gate, b_a, b_x, lam, *, rows):
    B, S, C = xr.shape
    perm, unperm = _scan_permutations(_SCAN_TILE)
    tok = pl.BlockSpec((None, rows, C), lambda b, i: (b, i, 0))
    row = lambda a: _resident(a.shape)
    return pl.pallas_call(
        _rglru_kernel,
        grid=(B, S // rows),
        in_specs=[tok, tok, row(perm), row(unperm), row(conv_w), row(conv_b), row(w_gate), row(b_a), row(b_x),
                  row(lam)],
        out_specs=tok,
        out_shape=jax.ShapeDtypeStruct((B, S, C), BF16),
        scratch_shapes=[pltpu.VMEM((V7X_SUBLANES, C), F32), pltpu.VMEM((V7X_SUBLANES, C), F32)],
        compiler_params=_params(("arbitrary", "arbitrary")),
        name="rg_lru",
    )(xr, gr, perm, unperm, conv_w, conv_b, w_gate, b_a, b_x, lam)


_KV_UNROLL = 4


def _attn_kernel(lam_ref, qt_ref, k_ref, vt_ref, g_ref, o_ref, qbd_ref, m_ref, l_ref, acc_ref,
                 *, tq, tk, online):
    def scores(j, masked):
        start = pl.multiple_of(j * tk, tk)
        s = _dot(k_ref[pl.ds(start, tk), :], qbd_ref[...])
        if masked:
            key = lax.broadcasted_iota(jnp.int32, s.shape, 0)
            col = lax.broadcasted_iota(jnp.int32, s.shape, 1)
            qry = jnp.where(col >= tq, col - tq, col)
            s = jnp.where(key <= qry, s, -jnp.inf)
        return s

    def colsum8(p):
        return jnp.sum(p.reshape(tk // V7X_SUBLANES, V7X_SUBLANES, 2 * tq), axis=0)

    def online_block(j, masked):
        s = scores(j, masked)
        m_old = m_ref[...]
        m_new = jnp.maximum(m_old, jnp.max(s, axis=0, keepdims=True))
        alpha = jnp.exp2(m_old - m_new)
        p = jnp.exp2(s - m_new)
        l_ref[...] = alpha * l_ref[...] + colsum8(p)
        acc_ref[...] = alpha * acc_ref[...] + _dot(vt_ref[j], p.astype(vt_ref.dtype))
        m_ref[...] = m_new

    def streaming_blocks(base, count, mask_last):
        lsum = pv = None
        for u in range(count):
            p = jnp.exp2(scores(base + u, mask_last and u == count - 1))
            ps = colsum8(p)
            d = _dot(vt_ref[base + u], p.astype(vt_ref.dtype))
            lsum = ps if lsum is None else lsum + ps
            pv = d if pv is None else pv + d
        l_ref[...] += lsum
        acc_ref[...] += pv

    def query_block(i, carry):
        zero = jnp.zeros((HEAD_DIM, tq), qbd_ref.dtype)
        qbd_ref[:HEAD_DIM, :tq] = qt_ref[i, :HEAD_DIM, :]
        qbd_ref[:HEAD_DIM, tq:] = zero
        qbd_ref[HEAD_DIM:, :tq] = zero
        qbd_ref[HEAD_DIM:, tq:] = qt_ref[i, HEAD_DIM:, :]
        if online:
            m_ref[...] = jnp.full(m_ref.shape, -jnp.inf, F32)
        l_ref[...] = jnp.zeros_like(l_ref)
        acc_ref[...] = jnp.zeros_like(acc_ref)

        if online:
            def body(j, c):
                online_block(j, False)
                return c

            lax.fori_loop(0, i, body, 0)
            online_block(i, True)
        else:
            def body(t, c):
                streaming_blocks(t * _KV_UNROLL, _KV_UNROLL, False)
                return c

            rem = lax.rem(i, _KV_UNROLL)
            lax.fori_loop(0, lax.div(i, _KV_UNROLL), body, 0)
            for r in range(_KV_UNROLL):
                @pl.when(rem == r)
                def _(r=r):
                    streaming_blocks(i - r, r + 1, True)

        inv_l = 1.0 / jnp.sum(l_ref[...], axis=0, keepdims=True)
        ot = acc_ref[:, :tq] * inv_l[:, :tq] - acc_ref[:, tq:] * (lam_ref[...] * inv_l[:, tq:])
        ms = jnp.mean(ot * ot, axis=0, keepdims=True)
        o_ref[i] = (ot * lax.rsqrt(ms + NORM_EPS) * g_ref[...]).astype(o_ref.dtype)
        return carry

    lax.fori_loop(0, qt_ref.shape[0], query_block, 0)


def _diff_attention(lam, qt, k, vt, subln_gain, *, online):
    B, nq, d_qk, tq = qt.shape
    S = k.shape[1]
    nkv, tk = vt.shape[1], vt.shape[3]
    assert tq == tk and lam.shape == (1, tq)
    hd2 = 2 * HEAD_DIM
    kernel = functools.partial(_attn_kernel, tq=tq, tk=tk, online=online)
    return pl.pallas_call(
        kernel,
        grid=(B, N_HEADS),
        in_specs=[
            pl.BlockSpec(lam.shape, lambda b, h: (0, 0)),
            pl.BlockSpec((None, nq, hd2, tq), lambda b, h: (b, 0, h, 0)),
            pl.BlockSpec((None, S, hd2), lambda b, h: (b, 0, h)),
            pl.BlockSpec((None, nkv, V_DIM, tk), lambda b, h: (b, 0, h, 0)),
            pl.BlockSpec((V_DIM, tq), lambda b, h: (0, 0)),
        ],
        out_specs=pl.BlockSpec((None, nq, V_DIM, tq), lambda b, h: (b, 0, h, 0)),
        out_shape=jax.ShapeDtypeStruct((B, nq, N_HEADS * V_DIM, tq), BF16),
        scratch_shapes=[
            pltpu.VMEM((hd2, 2 * tq), BF16),
            pltpu.VMEM((1, 2 * tq), F32),
            pltpu.VMEM((V7X_SUBLANES, 2 * tq), F32),
            pltpu.VMEM((V_DIM, 2 * tq), F32),
        ],
        compiler_params=_params(("arbitrary", "arbitrary")),
        name="diff_attention_online" if online else "diff_attention",
    )(lam, qt, k, vt, subln_gain)


def _post_kernel(x_ref, yr_ref, yat_ref, gm_ref, mod_ref, wpr_ref, wpa_ref, wo_ref, w1_ref, w2_ref, o_ref):
    D = x_ref.shape[-1]
    gates = jax.nn.sigmoid(gm_ref[...].astype(F32))
    merged = gates[:, :D] * _dot(yr_ref[...], wpr_ref[...]) + gates[:, D:] * _dot_tn(yat_ref[...], wpa_ref[...])
    x1 = x_ref[...] + mod_ref[2:3, :] * _dot(merged.astype(BF16), wo_ref[...])

    h2 = _ada_norm(x1, mod_ref[3:4, :], mod_ref[4:5, :]).astype(BF16)
    d_ff = w1_ref.shape[1]
    ff = None
    for c0 in range(0, d_ff, D):
        hid = jnp.square(jnp.maximum(_dot(h2, w1_ref[:, c0:c0 + D]), 0.0)).astype(BF16)
        part = _dot(hid, w2_ref[c0:c0 + D, :])
        ff = part if ff is None else ff + part
    o_ref[...] = x1 + mod_ref[5:6, :] * ff


def _post(x, y_rnn, y_attn_t, gm, mod3, wpr, wpa, wo, w1, w2, *, tm):
    B, S, D = x.shape
    tok = lambda w: pl.BlockSpec((None, tm, w), lambda b, i: (b, i, 0))
    return pl.pallas_call(
        _post_kernel,
        grid=(B, S // tm),
        in_specs=[
            tok(D), tok(y_rnn.shape[-1]),
            pl.BlockSpec((None, None, y_attn_t.shape[2], tm), lambda b, i: (b, i, 0, 0)),
            tok(gm.shape[-1]),
            pl.BlockSpec((None, N_MOD, D), lambda b, i: (b, 0, 0)),
            _resident(wpr.shape), _resident(wpa.shape), _resident(wo.shape),
            _resident(w1.shape), _resident(w2.shape),
        ],
        out_specs=tok(D),
        out_shape=jax.ShapeDtypeStruct((B, S, D), F32),
        compiler_params=_params(("arbitrary", "arbitrary")),
        name="post",
    )(x, y_rnn, y_attn_t, gm, mod3, wpr, wpa, wo, w1, w2)


def _token_tile(S, want):
    t = min(S, want)
    assert S % t == 0, (S, t)
    return t


def kernel(x, c, positions, w_ada, b_ada, w_in, conv_w, conv_b, rglru_wa, rglru_ba, rglru_wx, rglru_bx, rglru_lambda, q_norm_gain, k_norm_gain, lambda_q1, lambda_k1, lambda_q2, lambda_k2, subln_gain, w_proj_rnn, w_proj_attn, w_out, w_ff1, w_ff2):
    B, S, D = x.shape
    depth = w_in.shape[0]
    d_rnn = conv_w.shape[-1]
    d_qk = N_HEADS * 2 * HEAD_DIM
    d_attn = N_HEADS * V_DIM
    assert w_in.shape[-1] == 2 * d_rnn + 2 * d_qk + d_attn + N_BRANCH * D
    assert d_rnn == D and D % V7X_LANES == 0

    tm = _token_tile(S, 512)
    assert tm % _SCAN_TILE == 0
    scale = HEAD_DIM ** -0.5

    cos_t, sin_t = _rope_tables(positions)
    o1, o2, o3, o4, o5 = d_rnn, 2 * d_rnn, 2 * d_rnn + d_qk, 2 * d_rnn + 2 * d_qk, 2 * d_rnn + 2 * d_qk + d_attn

    for l in range(depth):
        lam_init = 0.8 - 0.6 * math.exp(-0.3 * l)
        lamv = jnp.stack([lambda_q1[l], lambda_k1[l], lambda_q2[l], lambda_k2[l]], axis=0)
        mod, lam = _modulation(c, w_ada[l], b_ada[l], lamv, lam_init=lam_init, lam_width=tm)
        mod3 = mod.reshape(B, N_MOD, D)

        wl = w_in[l]
        w_a = jnp.concatenate([wl[:, :o2], wl[:, o5:]], axis=1).astype(BF16)
        w_t = wl[:, o2:o5].T.astype(BF16)
        gq = jnp.broadcast_to((q_norm_gain[l] * (scale * _LOG2_E))[:, None], (HEAD_DIM, tm))
        gk = jnp.broadcast_to(k_norm_gain[l][:, None], (HEAD_DIM, tm))
        k_bound = math.sqrt(HEAD_DIM) * jnp.max(jnp.abs(k_norm_gain[l])) * _BOUND_SLACK
        q_bound = math.sqrt(HEAD_DIM) * jnp.max(jnp.abs(q_norm_gain[l])) * (scale * _LOG2_E) * _BOUND_SLACK
        xr, gr, gm, qt, k, vt = _in_projection(x, mod3, w_a, w_t, gq, gk, cos_t, sin_t, tm=tm, d_rnn=d_rnn)

        w_gate = jnp.concatenate([rglru_wa[l], rglru_wx[l]], axis=-1).astype(BF16)
        row = lambda v: v.reshape(1, -1)
        y_rnn = _rg_lru(xr, gr, conv_w[l], row(conv_b[l]), w_gate, row(rglru_ba[l]), row(rglru_bx[l]),
                        row(rglru_lambda[l]), rows=tm)

        attn = _diff_attention
        sub_gain = jnp.broadcast_to((subln_gain[l] * (1.0 - lam_init))[:, None], (V_DIM, tm))
        y_attn_t = lax.cond(
            q_bound * k_bound <= _MAX_ABS_SCORE_LOG2,
            functools.partial(attn, online=False), functools.partial(attn, online=True),
            lam, qt, k, vt, sub_gain)

        x = _post(x, y_rnn, y_attn_t, gm, mod3, w_proj_rnn[l].astype(BF16), w_proj_attn[l].astype(BF16),
                  w_out[l].astype(BF16), w_ff1[l].astype(BF16), w_ff2[l].astype(BF16), tm=tm)
    return x
```

```python
import functools
import math

import jax
import jax.numpy as jnp
from jax import lax
from jax.experimental import pallas as pl
from jax.experimental.pallas import tpu as pltpu

N_HEADS = 8
HEAD_DIM = 64
V_DIM = 2 * HEAD_DIM
ROPE_THETA = 10000.0
N_RNN_BLOCKS = 8
CONV_WIDTH = 4
LRU_C = 8.0
N_BRANCH = 2
N_MOD = 6
NORM_EPS = 1e-6

V7X_SUBLANES = 8
V7X_LANES = 128
V7X_VMEM_LIMIT_BYTES = 56 * 1024 * 1024

F32 = jnp.float32
BF16 = jnp.bfloat16

_LOG2_E = math.log2(math.e)
_BOUND_SLACK = 1.0 + 2.0 ** -6
_MAX_ABS_SCORE_LOG2 = 48.0


def _dot(a, b):
    return jnp.dot(a, b, preferred_element_type=F32)


def _dot_nt(a, b):
    return lax.dot_general(a, b, (((1,), (1,)), ((), ())), preferred_element_type=F32)


def _dot_tn(a, b):
    return lax.dot_general(a, b, (((0,), (0,)), ((), ())), preferred_element_type=F32)


def _resident(shape):
    zeros = (0,) * len(shape)
    return pl.BlockSpec(shape, lambda *_: zeros, pipeline_mode=pl.Buffered(1))


def _params(semantics):
    return pltpu.CompilerParams(dimension_semantics=semantics, vmem_limit_bytes=V7X_VMEM_LIMIT_BYTES)


def _mod_kernel(c_ref, w_ref, b_ref, lamv_ref, o_ref, lam_ref, *, lam_init):
    lv = lamv_ref[...]
    lam = (jnp.exp(jnp.sum(lv[0:1, :] * lv[1:2, :], axis=-1, keepdims=True))
           - jnp.exp(jnp.sum(lv[2:3, :] * lv[3:4, :], axis=-1, keepdims=True)) + lam_init)
    lam_ref[...] = jnp.broadcast_to(lam, lam_ref.shape)

    c = c_ref[...]
    ca = c * jax.nn.sigmoid(c)
    w = w_ref[...]
    ca_hi = ca.astype(BF16)
    ca_lo = (ca - ca_hi.astype(F32)).astype(BF16)
    w_hi = w.astype(BF16)
    w_lo = (w - w_hi.astype(F32)).astype(BF16)
    acc = _dot(ca_hi, w_hi) + _dot(ca_lo, w_hi) + _dot(ca_hi, w_lo)
    o_ref[...] = acc + b_ref[...]


def _modulation(c, w_ada, b_ada, lamv, *, lam_init, lam_width):
    B, D = c.shape
    n = w_ada.shape[1]
    tn = n // 4
    return pl.pallas_call(
        functools.partial(_mod_kernel, lam_init=lam_init),
        grid=(n // tn,),
        in_specs=[
            pl.BlockSpec((B, D), lambda j: (0, 0)),
            pl.BlockSpec((D, tn), lambda j: (0, j)),
            pl.BlockSpec((1, tn), lambda j: (0, j)),
            pl.BlockSpec(lamv.shape, lambda j: (0, 0)),
        ],
        out_specs=[pl.BlockSpec((B, tn), lambda j: (0, j)), pl.BlockSpec((1, lam_width), lambda j: (0, 0))],
        out_shape=[jax.ShapeDtypeStruct((B, n), F32), jax.ShapeDtypeStruct((1, lam_width), F32)],
        compiler_params=_params(("arbitrary",)),
        name="modulation",
    )(c, w_ada, b_ada.reshape(1, n), lamv)


def _ada_norm(x, shift, scale):
    ms = jnp.mean(x * x, axis=-1, keepdims=True)
    return (x * lax.rsqrt(ms + NORM_EPS)) * (1.0 + scale) + shift


def _norm_rope_fm(t, gain, cos, sin):
    half = HEAD_DIM // 2
    outs = []
    for c in range(2):
        blk = t[c * HEAD_DIM:(c + 1) * HEAD_DIM, :]
        ms = jnp.mean(blk * blk, axis=0, keepdims=True)
        n = blk * lax.rsqrt(ms + NORM_EPS) * gain
        x1 = n[:half, :]
        x2 = n[half:, :]
        outs.append(x1 * cos - x2 * sin)
        outs.append(x2 * cos + x1 * sin)
    return jnp.concatenate(outs, axis=0)


def _inproj_kernel(x_ref, mod_ref, wa_ref, wt_ref, gq_ref, gk_ref, pos_ref, invf_ref,
                   xr_ref, gr_ref, gm_ref, qt_ref, k_ref, vt_ref, *, d_rnn, d_qk, d_attn):
    h = _ada_norm(x_ref[...], mod_ref[0:1, :], mod_ref[1:2, :]).astype(BF16)

    def token_major(ref, col):
        for c0 in range(0, ref.shape[-1], d_rnn):
            ref[:, c0:c0 + d_rnn] = _dot(h, wa_ref[:, col + c0:col + c0 + d_rnn]).astype(ref.dtype)

    token_major(xr_ref, 0)
    token_major(gr_ref, d_rnn)
    q_raw = _dot_nt(wt_ref[0:d_qk, :], h)
    k_raw = _dot_nt(wt_ref[d_qk:2 * d_qk, :], h)
    token_major(gm_ref, 2 * d_rnn)
    vt_ref[...] = _dot_nt(wt_ref[2 * d_qk:2 * d_qk + d_attn, :], h).astype(vt_ref.dtype)

    ang = invf_ref[...] * pos_ref[...].astype(F32)
    cos = jnp.cos(ang)
    sin = jnp.sin(ang)
    hd2 = 2 * HEAD_DIM
    for hh in range(N_HEADS):
        rows = slice(hh * hd2, (hh + 1) * hd2)
        qt_ref[rows, :] = _norm_rope_fm(q_raw[rows, :], gq_ref[...], cos, sin).astype(qt_ref.dtype)
        kh = _norm_rope_fm(k_raw[rows, :], gk_ref[...], cos, sin)
        k_ref[:, rows] = kh.T.astype(k_ref.dtype)


def _in_projection(x, mod3, w_a, w_t, gq, gk, positions, *, tm, d_rnn):
    B, S, D = x.shape
    inv_freq = (ROPE_THETA ** (-jnp.arange(0, HEAD_DIM, 2, dtype=F32) / HEAD_DIM)).reshape(HEAD_DIM // 2, 1)
    d_qk = N_HEADS * 2 * HEAD_DIM
    d_attn = N_HEADS * V_DIM
    d_gm = N_BRANCH * D
    half = HEAD_DIM // 2
    nt = S // tm
    tok = lambda w: pl.BlockSpec((None, tm, w), lambda b, i: (b, i, 0))
    fm = lambda r: pl.BlockSpec((None, r, tm), lambda b, i: (b, 0, i))
    kernel = functools.partial(_inproj_kernel, d_rnn=d_rnn, d_qk=d_qk, d_attn=d_attn)
    return pl.pallas_call(
        kernel,
        grid=(B, nt),
        in_specs=[
            tok(D),
            pl.BlockSpec((None, N_MOD, D), lambda b, i: (b, 0, 0)),
            _resident(w_a.shape),
            _resident(w_t.shape),
            _resident(gq.shape),
            _resident(gk.shape),
            pl.BlockSpec((None, 1, tm), lambda b, i: (b, 0, i)),
            pl.BlockSpec((half, 1), lambda b, i: (0, 0)),
        ],
        out_specs=[
            tok(d_rnn), tok(d_rnn), tok(d_gm),
            pl.BlockSpec((None, None, d_qk, tm), lambda b, i: (b, i, 0, 0)),
            tok(d_qk),
            pl.BlockSpec((None, None, d_attn, tm), lambda b, i: (b, i, 0, 0)),
        ],
        out_shape=[
            jax.ShapeDtypeStruct((B, S, d_rnn), BF16),
            jax.ShapeDtypeStruct((B, S, d_rnn), BF16),
            jax.ShapeDtypeStruct((B, S, d_gm), BF16),
            jax.ShapeDtypeStruct((B, nt, d_qk, tm), BF16),
            jax.ShapeDtypeStruct((B, S, d_qk), BF16),
            jax.ShapeDtypeStruct((B, nt, d_attn, tm), BF16),
        ],
        compiler_params=_params(("arbitrary", "arbitrary")),
        name="in_projection",
    )(x, mod3, w_a, w_t, gq, gk, positions.reshape(B, 1, S), inv_freq)


_SCAN_TILE = 256


def _scan_permutations(ts):
    r = jnp.arange(ts)
    tau = (r % V7X_SUBLANES) * (ts // V7X_SUBLANES) + r // V7X_SUBLANES
    cols = jnp.arange(ts)[None, :]
    perm = jnp.stack([(cols == (tau[:, None] - d)).astype(BF16) for d in range(CONV_WIDTH)])
    return perm, perm[0].T


def _gelu_tanh(x):
    k2 = 2.0 * math.sqrt(2.0 / math.pi) * _LOG2_E
    e = jnp.exp2(x * ((-k2) - (k2 * 0.044715) * (x * x)))
    return x / (1.0 + e)


def _rglru_tile(xb, gb, tail, carry, perm_ref, unperm_ref, cw_ref, cb_ref, wg_ref, ba_ref, bx_ref, lam_ref):
    ts, C = xb.shape
    blk = C // N_RNN_BLOCKS
    ng = ts // V7X_SUBLANES
    sub = lax.broadcasted_iota(jnp.int32, (V7X_SUBLANES, C), 0)

    conv = cb_ref[...] + _dot(perm_ref[0], xb) * cw_ref[CONV_WIDTH - 1:CONV_WIDTH, :]
    for d in range(1, CONV_WIDTH):
        conv = conv + _dot(perm_ref[d], xb) * cw_ref[CONV_WIDTH - 1 - d:CONV_WIDTH - d, :]
    top = []
    for j in range(CONV_WIDTH - 1):
        fix = None
        for d in range(j + 1, CONV_WIDTH):
            t = tail[V7X_SUBLANES - d + j:V7X_SUBLANES - d + j + 1, :] * cw_ref[CONV_WIDTH - 1 - d:CONV_WIDTH - d, :]
            fix = t if fix is None else fix + t
        rows = conv[j * V7X_SUBLANES:(j + 1) * V7X_SUBLANES, :]
        top.append(jnp.where(sub == 0, rows + fix, rows))
    conv = jnp.concatenate(top + [conv[(CONV_WIDTH - 1) * V7X_SUBLANES:, :]], axis=0)

    convb = conv.astype(BF16)
    pre = [_dot(convb[:, n * blk:(n + 1) * blk], wg_ref[n]) for n in range(N_RNN_BLOCKS)]
    r = jax.nn.sigmoid(jnp.concatenate([p[:, :blk] for p in pre], axis=1) + ba_ref[...])
    i = jax.nn.sigmoid(jnp.concatenate([p[:, blk:] for p in pre], axis=1) + bx_ref[...])

    z = -lam_ref[...]
    softplus = jnp.maximum(z, 0.0) + jnp.log1p(jnp.exp(-jnp.abs(z)))
    log_a = (-LRU_C) * r * softplus
    a = jnp.exp(log_a)
    u = jnp.exp2(0.5 * jnp.log2(-jnp.tanh(log_a) * (a * a + 1.0))) * (i * conv)

    hl = pc = None
    hls, pcs = [], []
    for g in range(ng):
        sl = slice(g * V7X_SUBLANES, (g + 1) * V7X_SUBLANES)
        hl = u[sl, :] if hl is None else a[sl, :] * hl + u[sl, :]
        pc = a[sl, :] if pc is None else a[sl, :] * pc
        hls.append(hl)
        pcs.append(pc)
    f, gt = hl, pc
    s = 1
    while s < V7X_SUBLANES:
        ok = sub >= s
        f = jnp.where(ok, f + gt * pltpu.roll(f, s, 0), f)
        gt = jnp.where(ok, gt * pltpu.roll(gt, s, 0), gt)
        s *= 2
    after = f + gt * carry
    init = jnp.where(sub == 0, carry, pltpu.roll(after, 1, 0))
    new_carry = jnp.broadcast_to(after[V7X_SUBLANES - 1:, :], after.shape)
    h = jnp.concatenate([hls[g] + pcs[g] * init for g in range(ng)], axis=0)

    y = (h * _gelu_tanh(_dot(perm_ref[0], gb))).astype(BF16)
    return _dot(unperm_ref[...], y).astype(BF16), new_carry


def _rglru_kernel(xr_ref, gr_ref, perm_ref, unperm_ref, cw_ref, cb_ref, wg_ref, ba_ref, bx_ref, lam_ref,
                  y_ref, tail_ref, h_ref):
    rows = xr_ref.shape[0]
    ts = perm_ref.shape[-1]

    @pl.when(pl.program_id(1) == 0)
    def _():
        tail_ref[...] = jnp.zeros_like(tail_ref)
        h_ref[...] = jnp.zeros_like(h_ref)

    tail = tail_ref[...]
    carry = h_ref[...]
    for t0 in range(0, rows, ts):
        xb = xr_ref[t0:t0 + ts, :]
        y, carry = _rglru_tile(xb, gr_ref[t0:t0 + ts, :], tail, carry, perm_ref, unperm_ref,
                               cw_ref, cb_ref, wg_ref, ba_ref, bx_ref, lam_ref)
        y_ref[t0:t0 + ts, :] = y
        tail = xb[ts - V7X_SUBLANES:, :].astype(F32)
    tail_ref[...] = tail
    h_ref[...] = carry


def _rg_lru(xr, gr, conv_w, conv_b, w_gate, b_a, b_x, lam, *, rows):
    B, S, C = xr.shape
    perm, unperm = _scan_permutations(_SCAN_TILE)
    tok = pl.BlockSpec((None, rows, C), lambda b, i: (b, i, 0))
    row = lambda a: _resident(a.shape)
    return pl.pallas_call(
        _rglru_kernel,
        grid=(B, S // rows),
        in_specs=[tok, tok, row(perm), row(unperm), row(conv_w), row(conv_b), row(w_gate), row(b_a), row(b_x),
                  row(lam)],
        out_specs=tok,
        out_shape=jax.ShapeDtypeStruct((B, S, C), BF16),
        scratch_shapes=[pltpu.VMEM((V7X_SUBLANES, C), F32), pltpu.VMEM((V7X_SUBLANES, C), F32)],
        compiler_params=_params(("arbitrary", "arbitrary")),
        name="rg_lru",
    )(xr, gr, perm, unperm, conv_w, conv_b, w_gate, b_a, b_x, lam)


_KV_UNROLL = 8


def _attn_kernel(lam_ref, qt_ref, k_ref, vt_ref, g_ref, o_ref, qbd_ref, m_ref, l_ref, acc_ref,
                 *, tq, tk, online):
    def scores(j, masked):
        start = pl.multiple_of(j * tk, tk)
        s = _dot(k_ref[pl.ds(start, tk), :], qbd_ref[...])
        if masked:
            key = lax.broadcasted_iota(jnp.int32, s.shape, 0)
            col = lax.broadcasted_iota(jnp.int32, s.shape, 1)
            qry = jnp.where(col >= tq, col - tq, col)
            s = jnp.where(key <= qry, s, -jnp.inf)
        return s

    def colsum8(p):
        return jnp.sum(p.reshape(tk // V7X_SUBLANES, V7X_SUBLANES, 2 * tq), axis=0)

    def online_block(j, masked):
        s = scores(j, masked)
        m_old = m_ref[...]
        m_new = jnp.maximum(m_old, jnp.max(s, axis=0, keepdims=True))
        alpha = jnp.exp2(m_old - m_new)
        p = jnp.exp2(s - m_new)
        l_ref[...] = alpha * l_ref[...] + colsum8(p)
        acc_ref[...] = alpha * acc_ref[...] + _dot(vt_ref[j], p.astype(vt_ref.dtype))
        m_ref[...] = m_new

    def streaming_blocks(base, count, mask_last):
        lsum = pv = None
        for u in range(count):
            p = jnp.exp2(scores(base + u, mask_last and u == count - 1))
            ps = colsum8(p)
            d = _dot(vt_ref[base + u], p.astype(vt_ref.dtype))
            lsum = ps if lsum is None else lsum + ps
            pv = d if pv is None else pv + d
        l_ref[...] += lsum
        acc_ref[...] += pv

    def query_block(i, carry):
        zero = jnp.zeros((HEAD_DIM, tq), qbd_ref.dtype)
        qbd_ref[:HEAD_DIM, :tq] = qt_ref[i, :HEAD_DIM, :]
        qbd_ref[:HEAD_DIM, tq:] = zero
        qbd_ref[HEAD_DIM:, :tq] = zero
        qbd_ref[HEAD_DIM:, tq:] = qt_ref[i, HEAD_DIM:, :]
        if online:
            m_ref[...] = jnp.full(m_ref.shape, -jnp.inf, F32)
        l_ref[...] = jnp.zeros_like(l_ref)
        acc_ref[...] = jnp.zeros_like(acc_ref)

        if online:
            def body(j, c):
                online_block(j, False)
                return c

            lax.fori_loop(0, i, body, 0)
            online_block(i, True)
        else:
            def body(t, c):
                streaming_blocks(t * _KV_UNROLL, _KV_UNROLL, False)
                return c

            rem = lax.rem(i, _KV_UNROLL)
            lax.fori_loop(0, lax.div(i, _KV_UNROLL), body, 0)
            for r in range(_KV_UNROLL):
                @pl.when(rem == r)
                def _(r=r):
                    streaming_blocks(i - r, r + 1, True)

        inv_l = 1.0 / jnp.sum(l_ref[...], axis=0, keepdims=True)
        ot = acc_ref[:, :tq] * inv_l[:, :tq] - acc_ref[:, tq:] * (lam_ref[...] * inv_l[:, tq:])
        ms = jnp.mean(ot * ot, axis=0, keepdims=True)
        o_ref[i] = (ot * lax.rsqrt(ms + NORM_EPS) * g_ref[...]).astype(o_ref.dtype)
        return carry

    lax.fori_loop(0, qt_ref.shape[0], query_block, 0)


def _diff_attention(lam, qt, k, vt, subln_gain, *, online):
    B, nq, d_qk, tq = qt.shape
    S = k.shape[1]
    nkv, tk = vt.shape[1], vt.shape[3]
    assert tq == tk and lam.shape == (1, tq)
    hd2 = 2 * HEAD_DIM
    kernel = functools.partial(_attn_kernel, tq=tq, tk=tk, online=online)
    return pl.pallas_call(
        kernel,
        grid=(B, N_HEADS),
        in_specs=[
            pl.BlockSpec(lam.shape, lambda b, h: (0, 0)),
            pl.BlockSpec((None, nq, hd2, tq), lambda b, h: (b, 0, h, 0)),
            pl.BlockSpec((None, S, hd2), lambda b, h: (b, 0, h)),
            pl.BlockSpec((None, nkv, V_DIM, tk), lambda b, h: (b, 0, h, 0)),
            pl.BlockSpec((V_DIM, tq), lambda b, h: (0, 0)),
        ],
        out_specs=pl.BlockSpec((None, nq, V_DIM, tq), lambda b, h: (b, 0, h, 0)),
        out_shape=jax.ShapeDtypeStruct((B, nq, N_HEADS * V_DIM, tq), BF16),
        scratch_shapes=[
            pltpu.VMEM((hd2, 2 * tq), BF16),
            pltpu.VMEM((1, 2 * tq), F32),
            pltpu.VMEM((V7X_SUBLANES, 2 * tq), F32),
            pltpu.VMEM((V_DIM, 2 * tq), F32),
        ],
        compiler_params=_params(("arbitrary", "arbitrary")),
        name="diff_attention_online" if online else "diff_attention",
    )(lam, qt, k, vt, subln_gain)


def _post_kernel(x_ref, yr_ref, yat_ref, gm_ref, mod_ref, wpr_ref, wpa_ref, wo_ref, w1_ref, w2_ref, o_ref):
    D = x_ref.shape[-1]
    gates = jax.nn.sigmoid(gm_ref[...].astype(F32))
    merged = gates[:, :D] * _dot(yr_ref[...], wpr_ref[...]) + gates[:, D:] * _dot_tn(yat_ref[...], wpa_ref[...])
    x1 = x_ref[...] + mod_ref[2:3, :] * _dot(merged.astype(BF16), wo_ref[...])

    h2 = _ada_norm(x1, mod_ref[3:4, :], mod_ref[4:5, :]).astype(BF16)
    d_ff = w1_ref.shape[1]
    ff = None
    for c0 in range(0, d_ff, D):
        hid = jnp.square(jnp.maximum(_dot(h2, w1_ref[:, c0:c0 + D]), 0.0)).astype(BF16)
        part = _dot(hid, w2_ref[c0:c0 + D, :])
        ff = part if ff is None else ff + part
    o_ref[...] = x1 + mod_ref[5:6, :] * ff


def _post(x, y_rnn, y_attn_t, gm, mod3, wpr, wpa, wo, w1, w2, *, tm):
    B, S, D = x.shape
    tok = lambda w: pl.BlockSpec((None, tm, w), lambda b, i: (b, i, 0))
    return pl.pallas_call(
        _post_kernel,
        grid=(B, S // tm),
        in_specs=[
            tok(D), tok(y_rnn.shape[-1]),
            pl.BlockSpec((None, None, y_attn_t.shape[2], tm), lambda b, i: (b, i, 0, 0)),
            tok(gm.shape[-1]),
            pl.BlockSpec((None, N_MOD, D), lambda b, i: (b, 0, 0)),
            _resident(wpr.shape), _resident(wpa.shape), _resident(wo.shape),
            _resident(w1.shape), _resident(w2.shape),
        ],
        out_specs=tok(D),
        out_shape=jax.ShapeDtypeStruct((B, S, D), F32),
        compiler_params=_params(("arbitrary", "arbitrary")),
        name="post",
    )(x, y_rnn, y_attn_t, gm, mod3, wpr, wpa, wo, w1, w2)


def _token_tile(S, want):
    t = min(S, want)
    assert S % t == 0, (S, t)
    return t


def kernel(x, c, positions, w_ada, b_ada, w_in, conv_w, conv_b, rglru_wa, rglru_ba, rglru_wx, rglru_bx, rglru_lambda, q_norm_gain, k_norm_gain, lambda_q1, lambda_k1, lambda_q2, lambda_k2, subln_gain, w_proj_rnn, w_proj_attn, w_out, w_ff1, w_ff2):
    B, S, D = x.shape
    depth = w_in.shape[0]
    d_rnn = conv_w.shape[-1]
    d_qk = N_HEADS * 2 * HEAD_DIM
    d_attn = N_HEADS * V_DIM
    assert w_in.shape[-1] == 2 * d_rnn + 2 * d_qk + d_attn + N_BRANCH * D
    assert d_rnn == D and D % V7X_LANES == 0

    tm = _token_tile(S, 512)
    assert tm % _SCAN_TILE == 0
    scale = HEAD_DIM ** -0.5

    o1, o2, o3, o4, o5 = d_rnn, 2 * d_rnn, 2 * d_rnn + d_qk, 2 * d_rnn + 2 * d_qk, 2 * d_rnn + 2 * d_qk + d_attn

    for l in range(depth):
        lam_init = 0.8 - 0.6 * math.exp(-0.3 * l)
        lamv = jnp.stack([lambda_q1[l], lambda_k1[l], lambda_q2[l], lambda_k2[l]], axis=0)
        mod, lam = _modulation(c, w_ada[l], b_ada[l], lamv, lam_init=lam_init, lam_width=tm)
        mod3 = mod.reshape(B, N_MOD, D)

        wl = w_in[l]
        w_a = jnp.concatenate([wl[:, :o2], wl[:, o5:]], axis=1).astype(BF16)
        w_t = wl[:, o2:o5].T.astype(BF16)
        gq = jnp.broadcast_to((q_norm_gain[l] * (scale * _LOG2_E))[:, None], (HEAD_DIM, tm))
        gk = jnp.broadcast_to(k_norm_gain[l][:, None], (HEAD_DIM, tm))
        k_bound = math.sqrt(HEAD_DIM) * jnp.max(jnp.abs(k_norm_gain[l])) * _BOUND_SLACK
        q_bound = math.sqrt(HEAD_DIM) * jnp.max(jnp.abs(q_norm_gain[l])) * (scale * _LOG2_E) * _BOUND_SLACK
        xr, gr, gm, qt, k, vt = _in_projection(x, mod3, w_a, w_t, gq, gk, positions, tm=tm, d_rnn=d_rnn)

        w_gate = jnp.concatenate([rglru_wa[l], rglru_wx[l]], axis=-1).astype(BF16)
        row = lambda v: v.reshape(1, -1)
        y_rnn = _rg_lru(xr, gr, conv_w[l], row(conv_b[l]), w_gate, row(rglru_ba[l]), row(rglru_bx[l]),
                        row(rglru_lambda[l]), rows=_token_tile(S, 2 * tm))

        attn = _diff_attention
        sub_gain = jnp.broadcast_to((subln_gain[l] * (1.0 - lam_init))[:, None], (V_DIM, tm))
        y_attn_t = lax.cond(
            q_bound * k_bound <= _MAX_ABS_SCORE_LOG2,
            functools.partial(attn, online=False), functools.partial(attn, online=True),
            lam, qt, k, vt, sub_gain)

        x = _post(x, y_rnn, y_attn_t, gm, mod3, w_proj_rnn[l].astype(BF16), w_proj_attn[l].astype(BF16),
                  w_out[l].astype(BF16), w_ff1[l].astype(BF16), w_ff2[l].astype(BF16), tm=tm)
    return x
```

```python
import functools
import math

import jax
import jax.numpy as jnp
from jax import lax
from jax.experimental import pallas as pl
from jax.experimental.pallas import tpu as pltpu

N_HEADS = 8
HEAD_DIM = 64
V_DIM = 2 * HEAD_DIM
ROPE_THETA = 10000.0
N_RNN_BLOCKS = 8
CONV_WIDTH = 4
LRU_C = 8.0
N_BRANCH = 2
N_MOD = 6
NORM_EPS = 1e-6

V7X_SUBLANES = 8
V7X_LANES = 128
V7X_VMEM_LIMIT_BYTES = 56 * 1024 * 1024

F32 = jnp.float32
BF16 = jnp.bfloat16

_LOG2_E = math.log2(math.e)
_BOUND_SLACK = 1.0 + 2.0 ** -6
_MAX_ABS_SCORE_LOG2 = 48.0


def _dot(a, b):
    return jnp.dot(a, b, preferred_element_type=F32)


def _dot_nt(a, b):
    return lax.dot_general(a, b, (((1,), (1,)), ((), ())), preferred_element_type=F32)


def _dot_tn(a, b):
    return lax.dot_general(a, b, (((0,), (0,)), ((), ())), preferred_element_type=F32)


def _resident(shape):
    zeros = (0,) * len(shape)
    return pl.BlockSpec(shape, lambda *_: zeros, pipeline_mode=pl.Buffered(1))


def _params(semantics):
    return pltpu.CompilerParams(dimension_semantics=semantics, vmem_limit_bytes=V7X_VMEM_LIMIT_BYTES)


def _mod_kernel(c_ref, w_ref, b_ref, lamv_ref, o_ref, lam_ref, *, lam_init):
    lv = lamv_ref[...]
    lam = (jnp.exp(jnp.sum(lv[0:1, :] * lv[1:2, :], axis=-1, keepdims=True))
           - jnp.exp(jnp.sum(lv[2:3, :] * lv[3:4, :], axis=-1, keepdims=True)) + lam_init)
    lam_ref[...] = jnp.broadcast_to(lam, lam_ref.shape)

    c = c_ref[...]
    ca = c * jax.nn.sigmoid(c)
    w = w_ref[...]
    ca_hi = ca.astype(BF16)
    ca_lo = (ca - ca_hi.astype(F32)).astype(BF16)
    w_hi = w.astype(BF16)
    w_lo = (w - w_hi.astype(F32)).astype(BF16)
    acc = _dot(ca_hi, w_hi) + _dot(ca_lo, w_hi) + _dot(ca_hi, w_lo)
    o_ref[...] = acc + b_ref[...]


def _modulation(c, w_ada, b_ada, lamv, *, lam_init, lam_width):
    B, D = c.shape
    n = w_ada.shape[1]
    tn = n // 4
    return pl.pallas_call(
        functools.partial(_mod_kernel, lam_init=lam_init),
        grid=(n // tn,),
        in_specs=[
            pl.BlockSpec((B, D), lambda j: (0, 0)),
            pl.BlockSpec((D, tn), lambda j: (0, j)),
            pl.BlockSpec((1, tn), lambda j: (0, j)),
            pl.BlockSpec(lamv.shape, lambda j: (0, 0)),
        ],
        out_specs=[pl.BlockSpec((B, tn), lambda j: (0, j)), pl.BlockSpec((1, lam_width), lambda j: (0, 0))],
        out_shape=[jax.ShapeDtypeStruct((B, n), F32), jax.ShapeDtypeStruct((1, lam_width), F32)],
        compiler_params=_params(("arbitrary",)),
        name="modulation",
    )(c, w_ada, b_ada.reshape(1, n), lamv)


def _ada_norm(x, shift, scale):
    ms = jnp.mean(x * x, axis=-1, keepdims=True)
    return (x * lax.rsqrt(ms + NORM_EPS)) * (1.0 + scale) + shift


def _gained_rope_tables(gain, cos, sin):
    half = HEAD_DIM // 2
    g1, g2 = gain[:half, :], gain[half:, :]
    return g1 * cos, g1 * sin, g2 * cos, g2 * sin


def _norm_rope_fm(t, tables):
    half = HEAD_DIM // 2
    c1, s1, c2, s2 = tables
    outs = []
    for c in range(2):
        blk = t[c * HEAD_DIM:(c + 1) * HEAD_DIM, :]
        ms = jnp.mean(blk * blk, axis=0, keepdims=True)
        n = blk * lax.rsqrt(ms + NORM_EPS)
        x1 = n[:half, :]
        x2 = n[half:, :]
        outs.append(x1 * c1 - x2 * s2)
        outs.append(x2 * c2 + x1 * s1)
    return jnp.concatenate(outs, axis=0)


def _inproj_kernel(x_ref, mod_ref, wa_ref, wt_ref, gq_ref, gk_ref, pos_ref, invf_ref,
                   xr_ref, gr_ref, gm_ref, qt_ref, k_ref, vt_ref, *, d_rnn, d_qk, d_attn):
    h = _ada_norm(x_ref[...], mod_ref[0:1, :], mod_ref[1:2, :]).astype(BF16)

    def token_major(ref, col):
        for c0 in range(0, ref.shape[-1], d_rnn):
            ref[:, c0:c0 + d_rnn] = _dot(h, wa_ref[:, col + c0:col + c0 + d_rnn]).astype(ref.dtype)

    token_major(xr_ref, 0)
    token_major(gr_ref, d_rnn)
    q_raw = _dot_nt(wt_ref[0:d_qk, :], h)
    k_raw = _dot_nt(wt_ref[d_qk:2 * d_qk, :], h)
    token_major(gm_ref, 2 * d_rnn)
    vt_ref[...] = _dot_nt(wt_ref[2 * d_qk:2 * d_qk + d_attn, :], h).astype(vt_ref.dtype)

    ang = invf_ref[...] * pos_ref[...].astype(F32)
    cos = jnp.cos(ang)
    sin = jnp.sin(ang)
    q_tables = _gained_rope_tables(gq_ref[...], cos, sin)
    k_tables = _gained_rope_tables(gk_ref[...], cos, sin)
    hd2 = 2 * HEAD_DIM
    for hh in range(N_HEADS):
        rows = slice(hh * hd2, (hh + 1) * hd2)
        qt_ref[rows, :] = _norm_rope_fm(q_raw[rows, :], q_tables).astype(qt_ref.dtype)
        kh = _norm_rope_fm(k_raw[rows, :], k_tables)
        k_ref[:, rows] = kh.T.astype(k_ref.dtype)


def _in_projection(x, mod3, w_a, w_t, gq, gk, positions, *, tm, d_rnn):
    B, S, D = x.shape
    inv_freq = (ROPE_THETA ** (-jnp.arange(0, HEAD_DIM, 2, dtype=F32) / HEAD_DIM)).reshape(HEAD_DIM // 2, 1)
    d_qk = N_HEADS * 2 * HEAD_DIM
    d_attn = N_HEADS * V_DIM
    d_gm = N_BRANCH * D
    half = HEAD_DIM // 2
    nt = S // tm
    tok = lambda w: pl.BlockSpec((None, tm, w), lambda b, i: (b, i, 0))
    fm = lambda r: pl.BlockSpec((None, r, tm), lambda b, i: (b, 0, i))
    kernel = functools.partial(_inproj_kernel, d_rnn=d_rnn, d_qk=d_qk, d_attn=d_attn)
    return pl.pallas_call(
        kernel,
        grid=(B, nt),
        in_specs=[
            tok(D),
            pl.BlockSpec((None, N_MOD, D), lambda b, i: (b, 0, 0)),
            _resident(w_a.shape),
            _resident(w_t.shape),
            _resident(gq.shape),
            _resident(gk.shape),
            pl.BlockSpec((None, 1, tm), lambda b, i: (b, 0, i)),
            pl.BlockSpec((half, 1), lambda b, i: (0, 0)),
        ],
        out_specs=[
            tok(d_rnn), tok(d_rnn), tok(d_gm),
            pl.BlockSpec((None, None, d_qk, tm), lambda b, i: (b, i, 0, 0)),
            tok(d_qk),
            pl.BlockSpec((None, None, d_attn, tm), lambda b, i: (b, i, 0, 0)),
        ],
        out_shape=[
            jax.ShapeDtypeStruct((B, S, d_rnn), BF16),
            jax.ShapeDtypeStruct((B, S, d_rnn), BF16),
            jax.ShapeDtypeStruct((B, S, d_gm), BF16),
            jax.ShapeDtypeStruct((B, nt, d_qk, tm), BF16),
            jax.ShapeDtypeStruct((B, S, d_qk), BF16),
            jax.ShapeDtypeStruct((B, nt, d_attn, tm), BF16),
        ],
        compiler_params=_params(("arbitrary", "arbitrary")),
        name="in_projection",
    )(x, mod3, w_a, w_t, gq, gk, positions.reshape(B, 1, S), inv_freq)


_SCAN_TILE = 256


def _scan_permutations(ts):
    r = jnp.arange(ts)
    tau = (r % V7X_SUBLANES) * (ts // V7X_SUBLANES) + r // V7X_SUBLANES
    cols = jnp.arange(ts)[None, :]
    perm = jnp.stack([(cols == (tau[:, None] - d)).astype(BF16) for d in range(CONV_WIDTH)])
    return perm, perm[0].T


def _gelu_tanh(x):
    k2 = 2.0 * math.sqrt(2.0 / math.pi) * _LOG2_E
    e = jnp.exp2(x * ((-k2) - (k2 * 0.044715) * (x * x)))
    return x / (1.0 + e)


def _rglru_tile(xb, gb, tail, carry, perm_ref, unperm_ref, cw_ref, cb_ref, wg_ref, ba_ref, bx_ref, lam_ref):
    ts, C = xb.shape
    blk = C // N_RNN_BLOCKS
    ng = ts // V7X_SUBLANES
    sub = lax.broadcasted_iota(jnp.int32, (V7X_SUBLANES, C), 0)

    conv = cb_ref[...] + _dot(perm_ref[0], xb) * cw_ref[CONV_WIDTH - 1:CONV_WIDTH, :]
    for d in range(1, CONV_WIDTH):
        conv = conv + _dot(perm_ref[d], xb) * cw_ref[CONV_WIDTH - 1 - d:CONV_WIDTH - d, :]
    top = []
    for j in range(CONV_WIDTH - 1):
        fix = None
        for d in range(j + 1, CONV_WIDTH):
            t = tail[V7X_SUBLANES - d + j:V7X_SUBLANES - d + j + 1, :] * cw_ref[CONV_WIDTH - 1 - d:CONV_WIDTH - d, :]
            fix = t if fix is None else fix + t
        rows = conv[j * V7X_SUBLANES:(j + 1) * V7X_SUBLANES, :]
        top.append(jnp.where(sub == 0, rows + fix, rows))
    conv = jnp.concatenate(top + [conv[(CONV_WIDTH - 1) * V7X_SUBLANES:, :]], axis=0)

    convb = conv.astype(BF16)
    pre = [_dot(convb[:, n * blk:(n + 1) * blk], wg_ref[n]) for n in range(N_RNN_BLOCKS)]
    r = jax.nn.sigmoid(jnp.concatenate([p[:, :blk] for p in pre], axis=1) + ba_ref[...])
    i = jax.nn.sigmoid(jnp.concatenate([p[:, blk:] for p in pre], axis=1) + bx_ref[...])

    z = -lam_ref[...]
    softplus = jnp.maximum(z, 0.0) + jnp.log1p(jnp.exp(-jnp.abs(z)))
    log_a = (-LRU_C) * r * softplus
    a = jnp.exp(log_a)
    u = jnp.exp2(0.5 * jnp.log2(-jnp.tanh(log_a) * (a * a + 1.0))) * (i * conv)

    hl = pc = None
    hls, pcs = [], []
    for g in range(ng):
        sl = slice(g * V7X_SUBLANES, (g + 1) * V7X_SUBLANES)
        hl = u[sl, :] if hl is None else a[sl, :] * hl + u[sl, :]
        pc = a[sl, :] if pc is None else a[sl, :] * pc
        hls.append(hl)
        pcs.append(pc)
    f, gt = hl, pc
    s = 1
    while s < V7X_SUBLANES:
        ok = sub >= s
        f = jnp.where(ok, f + gt * pltpu.roll(f, s, 0), f)
        gt = jnp.where(ok, gt * pltpu.roll(gt, s, 0), gt)
        s *= 2
    after = f + gt * carry
    init = jnp.where(sub == 0, carry, pltpu.roll(after, 1, 0))
    new_carry = jnp.broadcast_to(after[V7X_SUBLANES - 1:, :], after.shape)
    h = jnp.concatenate([hls[g] + pcs[g] * init for g in range(ng)], axis=0)

    y = (h * _gelu_tanh(_dot(perm_ref[0], gb))).astype(BF16)
    return _dot(unperm_ref[...], y).astype(BF16), new_carry


def _rglru_kernel(xr_ref, gr_ref, perm_ref, unperm_ref, cw_ref, cb_ref, wg_ref, ba_ref, bx_ref, lam_ref,
                  y_ref, tail_ref, h_ref):
    rows = xr_ref.shape[0]
    ts = perm_ref.shape[-1]

    @pl.when(pl.program_id(1) == 0)
    def _():
        tail_ref[...] = jnp.zeros_like(tail_ref)
        h_ref[...] = jnp.zeros_like(h_ref)

    tail = tail_ref[...]
    carry = h_ref[...]
    for t0 in range(0, rows, ts):
        xb = xr_ref[t0:t0 + ts, :]
        y, carry = _rglru_tile(xb, gr_ref[t0:t0 + ts, :], tail, carry, perm_ref, unperm_ref,
                               cw_ref, cb_ref, wg_ref, ba_ref, bx_ref, lam_ref)
        y_ref[t0:t0 + ts, :] = y
        tail = xb[ts - V7X_SUBLANES:, :].astype(F32)
    tail_ref[...] = tail
    h_ref[...] = carry


def _rg_lru(xr, gr, conv_w, conv_b, w_gate, b_a, b_x, lam, *, rows):
    B, S, C = xr.shape
    perm, unperm = _scan_permutations(_SCAN_TILE)
    tok = pl.BlockSpec((None, rows, C), lambda b, i: (b, i, 0))
    row = lambda a: _resident(a.shape)
    return pl.pallas_call(
        _rglru_kernel,
        grid=(B, S // rows),
        in_specs=[tok, tok, row(perm), row(unperm), row(conv_w), row(conv_b), row(w_gate), row(b_a), row(b_x),
                  row(lam)],
        out_specs=tok,
        out_shape=jax.ShapeDtypeStruct((B, S, C), BF16),
        scratch_shapes=[pltpu.VMEM((V7X_SUBLANES, C), F32), pltpu.VMEM((V7X_SUBLANES, C), F32)],
        compiler_params=_params(("arbitrary", "arbitrary")),
        name="rg_lru",
    )(xr, gr, perm, unperm, conv_w, conv_b, w_gate, b_a, b_x, lam)


_KV_UNROLL = 8


def _attn_kernel(lam_ref, qt_ref, k_ref, vt_ref, g_ref, o_ref, qbd_ref, m_ref, l_ref, acc_ref,
                 *, tq, tk, online):
    def scores(j, masked):
        start = pl.multiple_of(j * tk, tk)
        s = _dot(k_ref[pl.ds(start, tk), :], qbd_ref[...])
        if masked:
            key = lax.broadcasted_iota(jnp.int32, s.shape, 0)
            col = lax.broadcasted_iota(jnp.int32, s.shape, 1)
            qry = jnp.where(col >= tq, col - tq, col)
            s = jnp.where(key <= qry, s, -jnp.inf)
        return s

    def colsum8(p):
        return jnp.sum(p.reshape(tk // V7X_SUBLANES, V7X_SUBLANES, 2 * tq), axis=0)

    def online_block(j, masked):
        s = scores(j, masked)
        m_old = m_ref[...]
        m_new = jnp.maximum(m_old, jnp.max(s, axis=0, keepdims=True))
        alpha = jnp.exp2(m_old - m_new)
        p = jnp.exp2(s - m_new)
        l_ref[...] = alpha * l_ref[...] + colsum8(p)
        acc_ref[...] = alpha * acc_ref[...] + _dot(vt_ref[j], p.astype(vt_ref.dtype))
        m_ref[...] = m_new

    def streaming_blocks(base, count, mask_last):
        lsum = pv = None
        for u in range(count):
            p = jnp.exp2(scores(base + u, mask_last and u == count - 1))
            ps = colsum8(p)
            d = _dot(vt_ref[base + u], p.astype(vt_ref.dtype))
            lsum = ps if lsum is None else lsum + ps
            pv = d if pv is None else pv + d
        l_ref[...] += lsum
        acc_ref[...] += pv

    def query_block(i, carry):
        zero = jnp.zeros((HEAD_DIM, tq), qbd_ref.dtype)
        qbd_ref[:HEAD_DIM, :tq] = qt_ref[i, :HEAD_DIM, :]
        qbd_ref[:HEAD_DIM, tq:] = zero
        qbd_ref[HEAD_DIM:, :tq] = zero
        qbd_ref[HEAD_DIM:, tq:] = qt_ref[i, HEAD_DIM:, :]
        if online:
            m_ref[...] = jnp.full(m_ref.shape, -jnp.inf, F32)
        l_ref[...] = jnp.zeros_like(l_ref)
        acc_ref[...] = jnp.zeros_like(acc_ref)

        if online:
            def body(j, c):
                online_block(j, False)
                return c

            lax.fori_loop(0, i, body, 0)
            online_block(i, True)
        else:
            def body(t, c):
                streaming_blocks(t * _KV_UNROLL, _KV_UNROLL, False)
                return c

            rem = lax.rem(i, _KV_UNROLL)
            lax.fori_loop(0, lax.div(i, _KV_UNROLL), body, 0)
            for r in range(_KV_UNROLL):
                @pl.when(rem == r)
                def _(r=r):
                    streaming_blocks(i - r, r + 1, True)

        inv_l = 1.0 / jnp.sum(l_ref[...], axis=0, keepdims=True)
        ot = acc_ref[:, :tq] * inv_l[:, :tq] - acc_ref[:, tq:] * (lam_ref[...] * inv_l[:, tq:])
        ms = jnp.mean(ot * ot, axis=0, keepdims=True)
        o_ref[i] = (ot * lax.rsqrt(ms + NORM_EPS) * g_ref[...]).astype(o_ref.dtype)
        return carry

    lax.fori_loop(0, qt_ref.shape[0], query_block, 0)


def _diff_attention(lam, qt, k, vt, subln_gain, *, online):
    B, nq, d_qk, tq = qt.shape
    S = k.shape[1]
    nkv, tk = vt.shape[1], vt.shape[3]
    assert tq == tk and lam.shape == (1, tq)
    hd2 = 2 * HEAD_DIM
    kernel = functools.partial(_attn_kernel, tq=tq, tk=tk, online=online)
    return pl.pallas_call(
        kernel,
        grid=(B, N_HEADS),
        in_specs=[
            pl.BlockSpec(lam.shape, lambda b, h: (0, 0)),
            pl.BlockSpec((None, nq, hd2, tq), lambda b, h: (b, 0, h, 0)),
            pl.BlockSpec((None, S, hd2), lambda b, h: (b, 0, h)),
            pl.BlockSpec((None, nkv, V_DIM, tk), lambda b, h: (b, 0, h, 0)),
            pl.BlockSpec((V_DIM, tq), lambda b, h: (0, 0)),
        ],
        out_specs=pl.BlockSpec((None, nq, V_DIM, tq), lambda b, h: (b, 0, h, 0)),
        out_shape=jax.ShapeDtypeStruct((B, nq, N_HEADS * V_DIM, tq), BF16),
        scratch_shapes=[
            pltpu.VMEM((hd2, 2 * tq), BF16),
            pltpu.VMEM((1, 2 * tq), F32),
            pltpu.VMEM((V7X_SUBLANES, 2 * tq), F32),
            pltpu.VMEM((V_DIM, 2 * tq), F32),
        ],
        compiler_params=_params(("arbitrary", "arbitrary")),
        name="diff_attention_online" if online else "diff_attention",
    )(lam, qt, k, vt, subln_gain)


def _post_kernel(x_ref, yr_ref, yat_ref, gm_ref, mod_ref, wpr_ref, wpa_ref, wo_ref, w1_ref, w2_ref, o_ref):
    D = x_ref.shape[-1]
    gates = jax.nn.sigmoid(gm_ref[...].astype(F32))
    merged = gates[:, :D] * _dot(yr_ref[...], wpr_ref[...]) + gates[:, D:] * _dot_tn(yat_ref[...], wpa_ref[...])
    x1 = x_ref[...] + mod_ref[2:3, :] * _dot(merged.astype(BF16), wo_ref[...])

    h2 = _ada_norm(x1, mod_ref[3:4, :], mod_ref[4:5, :]).astype(BF16)
    d_ff = w1_ref.shape[1]
    ff = None
    for c0 in range(0, d_ff, D):
        hid = jnp.square(jnp.maximum(_dot(h2, w1_ref[:, c0:c0 + D]), 0.0)).astype(BF16)
        part = _dot(hid, w2_ref[c0:c0 + D, :])
        ff = part if ff is None else ff + part
    o_ref[...] = x1 + mod_ref[5:6, :] * ff


def _post(x, y_rnn, y_attn_t, gm, mod3, wpr, wpa, wo, w1, w2, *, tm):
    B, S, D = x.shape
    tok = lambda w: pl.BlockSpec((None, tm, w), lambda b, i: (b, i, 0))
    return pl.pallas_call(
        _post_kernel,
        grid=(B, S // tm),
        in_specs=[
            tok(D), tok(y_rnn.shape[-1]),
            pl.BlockSpec((None, None, y_attn_t.shape[2], tm), lambda b, i: (b, i, 0, 0)),
            tok(gm.shape[-1]),
            pl.BlockSpec((None, N_MOD, D), lambda b, i: (b, 0, 0)),
            _resident(wpr.shape), _resident(wpa.shape), _resident(wo.shape),
            _resident(w1.shape), _resident(w2.shape),
        ],
        out_specs=tok(D),
        out_shape=jax.ShapeDtypeStruct((B, S, D), F32),
        compiler_params=_params(("arbitrary", "arbitrary")),
        name="post",
    )(x, y_rnn, y_attn_t, gm, mod3, wpr, wpa, wo, w1, w2)


def _token_tile(S, want):
    t = min(S, want)
    assert S % t == 0, (S, t)
    return t


def kernel(x, c, positions, w_ada, b_ada, w_in, conv_w, conv_b, rglru_wa, rglru_ba, rglru_wx, rglru_bx, rglru_lambda, q_norm_gain, k_norm_gain, lambda_q1, lambda_k1, lambda_q2, lambda_k2, subln_gain, w_proj_rnn, w_proj_attn, w_out, w_ff1, w_ff2):
    B, S, D = x.shape
    depth = w_in.shape[0]
    d_rnn = conv_w.shape[-1]
    d_qk = N_HEADS * 2 * HEAD_DIM
    d_attn = N_HEADS * V_DIM
    assert w_in.shape[-1] == 2 * d_rnn + 2 * d_qk + d_attn + N_BRANCH * D
    assert d_rnn == D and D % V7X_LANES == 0

    tm = _token_tile(S, 512)
    assert tm % _SCAN_TILE == 0
    scale = HEAD_DIM ** -0.5

    o1, o2, o3, o4, o5 = d_rnn, 2 * d_rnn, 2 * d_rnn + d_qk, 2 * d_rnn + 2 * d_qk, 2 * d_rnn + 2 * d_qk + d_attn

    for l in range(depth):
        lam_init = 0.8 - 0.6 * math.exp(-0.3 * l)
        lamv = jnp.stack([lambda_q1[l], lambda_k1[l], lambda_q2[l], lambda_k2[l]], axis=0)
        mod, lam = _modulation(c, w_ada[l], b_ada[l], lamv, lam_init=lam_init, lam_width=tm)
        mod3 = mod.reshape(B, N_MOD, D)

        wl = w_in[l]
        w_a = jnp.concatenate([wl[:, :o2], wl[:, o5:]], axis=1).astype(BF16)
        w_t = wl[:, o2:o5].T.astype(BF16)
        gq = jnp.broadcast_to((q_norm_gain[l] * (scale * _LOG2_E))[:, None], (HEAD_DIM, tm))
        gk = jnp.broadcast_to(k_norm_gain[l][:, None], (HEAD_DIM, tm))
        k_bound = math.sqrt(HEAD_DIM) * jnp.max(jnp.abs(k_norm_gain[l])) * _BOUND_SLACK
        q_bound = math.sqrt(HEAD_DIM) * jnp.max(jnp.abs(q_norm_gain[l])) * (scale * _LOG2_E) * _BOUND_SLACK
        xr, gr, gm, qt, k, vt = _in_projection(x, mod3, w_a, w_t, gq, gk, positions, tm=tm, d_rnn=d_rnn)

        w_gate = jnp.concatenate([rglru_wa[l], rglru_wx[l]], axis=-1).astype(BF16)
        row = lambda v: v.reshape(1, -1)
        y_rnn = _rg_lru(xr, gr, conv_w[l], row(conv_b[l]), w_gate, row(rglru_ba[l]), row(rglru_bx[l]),
                        row(rglru_lambda[l]), rows=_token_tile(S, 2 * tm))

        attn = _diff_attention
        sub_gain = jnp.broadcast_to((subln_gain[l] * (1.0 - lam_init))[:, None], (V_DIM, tm))
        y_attn_t = lax.cond(
            q_bound * k_bound <= _MAX_ABS_SCORE_LOG2,
            functools.partial(attn, online=False), functools.partial(attn, online=True),
            lam, qt, k, vt, sub_gain)

        x = _post(x, y_rnn, y_attn_t, gm, mod3, w_proj_rnn[l].astype(BF16), w_proj_attn[l].astype(BF16),
                  w_out[l].astype(BF16), w_ff1[l].astype(BF16), w_ff2[l].astype(BF16), tm=tm)
    return x
```
